```python
import math
import jax, jax.numpy as jnp
from jax import lax
import numpy as np


D_MODEL = 4096
BATCH = 4
SEQ = 4096
DEPTH = 1

HEAD_DIM_DIFF = 128
N_HEADS_DIFF = (D_MODEL // 2) // (2 * HEAD_DIM_DIFF)
DIFF_WIDTH = N_HEADS_DIFF * 2 * HEAD_DIM_DIFF
HEAD_DIM_SWA = 64
N_HEADS_SWA = (D_MODEL // 2) // HEAD_DIM_SWA
SWA_GROUP = 8
N_KV_SWA = N_HEADS_SWA // SWA_GROUP
SWA_WIDTH = N_HEADS_SWA * HEAD_DIM_SWA
MIX_WIDTH = DIFF_WIDTH + SWA_WIDTH
WINDOW = 128
BLOCK = 128

COL_SIZES = [DIFF_WIDTH, DIFF_WIDTH, DIFF_WIDTH,
             SWA_WIDTH, N_KV_SWA * HEAD_DIM_SWA, N_KV_SWA * HEAD_DIM_SWA]
COL_SPLITS = [int(v) for v in np.cumsum(COL_SIZES)[:-1]]
IN_WIDTH = int(sum(COL_SIZES))

N_BUCKETS = 32
MAX_DISTANCE = 128
N_BIAS_HEADS = N_HEADS_DIFF + N_HEADS_SWA

N_EXPERTS = 32
TOP_K = 4
D_FF_EXPERT = D_MODEL // 4
SWIGLU_ALPHA = 1.702
SWIGLU_LIMIT = 7.0

LN_EPS = 1e-5
RMS_EPS = 1e-5
DEEPNORM_ALPHA = (2.0 * DEPTH) ** 0.25
DEEPNORM_BETA = (8.0 * DEPTH) ** -0.25

kernel_name = 'hybrid_diffattn_swa_sink_moe_deepnorm_adaln'

F32 = jnp.float32


def t5_causal_bucket(dist):
    n = jnp.maximum(dist, 0)
    max_exact = N_BUCKETS // 2
    nf = jnp.maximum(n, 1).astype(F32)
    large = max_exact + (jnp.log(nf / max_exact) / math.log(MAX_DISTANCE / max_exact)
                         * (N_BUCKETS - max_exact)).astype(jnp.int32)
    large = jnp.minimum(large, N_BUCKETS - 1)
    return jnp.where(n < max_exact, n, large)


def layer_norm(h, g, b):
    h32 = h.astype(F32)
    mu = jnp.mean(h32, axis=-1, keepdims=True)
    var = jnp.mean(jnp.square(h32 - mu), axis=-1, keepdims=True)
    return ((h32 - mu) * lax.rsqrt(var + LN_EPS) * g + b).astype(h.dtype)


def diff_attention(q, k, v, table, lam, subln_g, lambda_init):
    B, S = q.shape[0], q.shape[1]
    nb = S // BLOCK
    scale = HEAD_DIM_DIFF ** -0.5
    qb = q.reshape(B, nb, BLOCK, N_HEADS_DIFF, 2, HEAD_DIM_DIFF).transpose(1, 0, 2, 3, 4, 5)
    kpos = jnp.arange(S)

    def one_block(args):
        i, qi = args
        qpos = i * BLOCK + jnp.arange(BLOCK)
        dist = qpos[:, None] - kpos[None, :]
        bias = jnp.transpose(table[t5_causal_bucket(dist)], (2, 0, 1)).astype(F32)
        logits = jnp.einsum('bqhcd,bkhcd->bhcqk', qi, k).astype(F32) * scale + bias[None, :, None]
        logits = jnp.where((dist >= 0)[None, None, None], logits, -jnp.inf)
        p = jax.nn.softmax(logits, axis=-1)
        attn = p[:, :, 0] - lam * p[:, :, 1]
        return jnp.einsum('bhqk,bkhe->bqhe', attn.astype(v.dtype), v)

    o = lax.map(one_block, (jnp.arange(nb), qb))
    o = o.transpose(1, 0, 2, 3, 4).reshape(B, S, N_HEADS_DIFF, 2 * HEAD_DIM_DIFF)
    o32 = o.astype(F32)
    o32 = o32 * lax.rsqrt(jnp.mean(jnp.square(o32), axis=-1, keepdims=True) + RMS_EPS) * subln_g
    o32 = o32 * (1.0 - lambda_init)
    return o32.reshape(B, S, DIFF_WIDTH).astype(v.dtype)


def swa_sink_attention(q, k, v, table, sinks):
    B, S = q.shape[0], q.shape[1]
    nb = S // BLOCK
    scale = HEAD_DIM_SWA ** -0.5
    qb = q.reshape(B, nb, BLOCK, N_KV_SWA, SWA_GROUP, HEAD_DIM_SWA)
    kb = k.reshape(B, nb, BLOCK, N_KV_SWA, HEAD_DIM_SWA)
    vb = v.reshape(B, nb, BLOCK, N_KV_SWA, HEAD_DIM_SWA)

    def band(t):
        prev = jnp.pad(t, ((0, 0), (1, 0), (0, 0), (0, 0), (0, 0)))[:, :-1]
        return jnp.concatenate([prev, t], axis=2)

    kk, vv = band(kb), band(vb)
    qi = jnp.arange(BLOCK)
    kj = jnp.arange(2 * BLOCK)
    dist = BLOCK + qi[:, None] - kj[None, :]
    in_window = (dist >= 0) & (dist < WINDOW)
    before_start = (jnp.arange(nb)[:, None, None] == 0) & (kj[None, None, :] < BLOCK)
    mask = in_window[None] & jnp.logical_not(before_start)
    bias = jnp.transpose(table[t5_causal_bucket(dist)], (2, 0, 1)).astype(F32)
    bias = bias.reshape(N_KV_SWA, SWA_GROUP, BLOCK, 2 * BLOCK)
    logits = jnp.einsum('bnqhgd,bnkhd->bnhgqk', qb, kk).astype(F32) * scale + bias[None, None]
    logits = jnp.where(mask[None, :, None, None], logits, -jnp.inf)
    sink = jnp.broadcast_to(sinks.astype(F32).reshape(1, 1, N_KV_SWA, SWA_GROUP, 1, 1),
                            logits.shape[:-1] + (1,))
    p = jax.nn.softmax(jnp.concatenate([logits, sink], axis=-1), axis=-1)[..., :-1]
    o = jnp.einsum('bnhgqk,bnkhd->bnqhgd', p.astype(v.dtype), vv)
    return o.reshape(B, S, SWA_WIDTH)


def hybrid_mixer(u, w_in, b_in, rel_bias, lambda_q1, lambda_k1, lambda_q2, lambda_k2,
                 subln_g, sinks, w_out, b_out, lambda_init):
    B, S, _ = u.shape
    proj = u @ w_in + b_in
    qd, kd, vd, qs, ks, vs = jnp.split(proj, COL_SPLITS, axis=-1)
    lam = (jnp.exp(jnp.sum(lambda_q1.astype(F32) * lambda_k1.astype(F32)))
           - jnp.exp(jnp.sum(lambda_q2.astype(F32) * lambda_k2.astype(F32))) + lambda_init)
    qd = qd.reshape(B, S, N_HEADS_DIFF, 2, HEAD_DIM_DIFF)
    kd = kd.reshape(B, S, N_HEADS_DIFF, 2, HEAD_DIM_DIFF)
    vd = vd.reshape(B, S, N_HEADS_DIFF, 2 * HEAD_DIM_DIFF)
    out_a = diff_attention(qd, kd, vd, rel_bias[:, :N_HEADS_DIFF], lam, subln_g, lambda_init)
    qs = qs.reshape(B, S, N_HEADS_SWA, HEAD_DIM_SWA)
    ks = ks.reshape(B, S, N_KV_SWA, HEAD_DIM_SWA)
    vs = vs.reshape(B, S, N_KV_SWA, HEAD_DIM_SWA)
    out_b = swa_sink_attention(qs, ks, vs, rel_bias[:, N_HEADS_DIFF:], sinks)
    merged = jnp.concatenate([out_a, out_b.astype(out_a.dtype)], axis=-1)
    return merged @ w_out + b_out


def clamped_swiglu(h):
    glu, lin = h[..., :D_FF_EXPERT], h[..., D_FF_EXPERT:]
    glu = jnp.minimum(glu, SWIGLU_LIMIT)
    lin = jnp.clip(lin, -SWIGLU_LIMIT, SWIGLU_LIMIT)
    return glu * jax.nn.sigmoid(SWIGLU_ALPHA * glu) * (lin + 1.0)


def moe(u, w_router, b_router, w1, b1, w2, b2):
    B, S, D = u.shape
    t = u.reshape(B * S, D)
    logits = (t @ w_router + b_router).astype(F32)
    top_val, top_idx = lax.top_k(logits, TOP_K)
    gates = jax.nn.softmax(top_val, axis=-1)
    dense_gate = jnp.sum(jax.nn.one_hot(top_idx, N_EXPERTS, dtype=F32) * gates[..., None], axis=1)
    out = jnp.zeros((B * S, D), F32)
    for e in range(N_EXPERTS):
        h = t @ w1[e] + b1[e]
        y = clamped_swiglu(h) @ w2[e] + b2[e]
        out = out + dense_gate[:, e:e + 1] * y
    return out.reshape(B, S, D).astype(u.dtype)


def setup_inputs(seed: int = 0) -> dict:
    key = jax.random.key(seed)
    ks = jax.random.split(key, 32)
    L, D, E, F = DEPTH, D_MODEL, N_EXPERTS, D_FF_EXPERT

    def nrm(k, shape, s):
        return jax.random.normal(k, shape, F32) * s

    return {
        'x': nrm(ks[0], (BATCH, SEQ, D), 1.0),
        'c': nrm(ks[1], (BATCH, D), 1.0),
        'rel_bias': nrm(ks[2], (N_BUCKETS, N_BIAS_HEADS), 0.1),
        'w_mod': nrm(ks[3], (L, D, 6 * D), 0.1 * D ** -0.5),
        'b_mod': nrm(ks[4], (L, 6 * D), 0.01),
        'w_in': nrm(ks[5], (L, D, IN_WIDTH), D ** -0.5),
        'b_in': nrm(ks[6], (L, IN_WIDTH), 0.01),
        'lambda_q1': nrm(ks[7], (L, HEAD_DIM_DIFF), 0.1),
        'lambda_k1': nrm(ks[8], (L, HEAD_DIM_DIFF), 0.1),
        'lambda_q2': nrm(ks[9], (L, HEAD_DIM_DIFF), 0.1),
        'lambda_k2': nrm(ks[10], (L, HEAD_DIM_DIFF), 0.1),
        'subln_g': 1.0 + nrm(ks[11], (L, 2 * HEAD_DIM_DIFF), 0.02),
        'sinks': nrm(ks[12], (L, N_HEADS_SWA), 0.5),
        'w_out': nrm(ks[13], (L, MIX_WIDTH, D), DEEPNORM_BETA * MIX_WIDTH ** -0.5),
        'b_out': nrm(ks[14], (L, D), 0.01),
        'ln1_g': 1.0 + nrm(ks[15], (L, D), 0.02),
        'ln1_b': nrm(ks[16], (L, D), 0.01),
        'w_router': nrm(ks[17], (L, D, E), D ** -0.5),
        'b_router': nrm(ks[18], (L, E), 0.01),
        'w1': nrm(ks[19], (L, E, D, 2 * F), D ** -0.5),
        'b1': nrm(ks[20], (L, E, 2 * F), 0.01),
        'w2': nrm(ks[21], (L, E, F, D), DEEPNORM_BETA * F ** -0.5),
        'b2': nrm(ks[22], (L, E, D), 0.01),
        'ln2_g': 1.0 + nrm(ks[23], (L, D), 0.02),
        'ln2_b': nrm(ks[24], (L, D), 0.01),
    }


def reference(x, c, rel_bias, w_mod, b_mod, w_in, b_in, lambda_q1, lambda_k1, lambda_q2, lambda_k2,
              subln_g, sinks, w_out, b_out, ln1_g, ln1_b, w_router, b_router, w1, b1, w2, b2,
              ln2_g, ln2_b):
    B, D = c.shape
    for l in range(DEPTH):
        lambda_init = 0.8 - 0.6 * math.exp(-0.3 * l)
        mod = (jax.nn.silu(c) @ w_mod[l] + b_mod[l]).reshape(B, 6, D)[:, :, None, :]
        shift1, scale1, gate1, shift2, scale2, gate2 = [mod[:, i] for i in range(6)]
        u = x * (1.0 + scale1) + shift1
        a = hybrid_mixer(u, w_in[l], b_in[l], rel_bias, lambda_q1[l], lambda_k1[l], lambda_q2[l],
                         lambda_k2[l], subln_g[l], sinks[l], w_out[l], b_out[l], lambda_init)
        x = layer_norm(DEEPNORM_ALPHA * x + (1.0 + gate1) * a, ln1_g[l], ln1_b[l])
        u = x * (1.0 + scale2) + shift2
        f = moe(u, w_router[l], b_router[l], w1[l], b1[l], w2[l], b2[l])
        x = layer_norm(DEEPNORM_ALPHA * x + (1.0 + gate2) * f, ln2_g[l], ln2_b[l])
    return x
```

```python
import functools
import math

import jax
import jax.numpy as jnp
from jax import lax
from jax.experimental import pallas as pl
from jax.experimental.pallas import tpu as pltpu

F32 = jnp.float32
BF16 = jnp.bfloat16
I32 = jnp.int32

HEAD_DIM_DIFF = 128
HEAD_DIM_SWA = 64
SWA_GROUP = 8
WINDOW = 128
N_BUCKETS = 32
MAX_DISTANCE = 128
TOP_K = 4
SWIGLU_ALPHA = 1.702
SWIGLU_LIMIT = 7.0
LN_EPS = 1e-5
RMS_EPS = 1e-5
DEPTH = 1
DEEPNORM_ALPHA = (2.0 * DEPTH) ** 0.25
NEG_BIG = -1e30

LANES = 128
VMEM_LIMIT = 56 * 1024 * 1024

ATT_TILE = 512
PROJ_TM = 512
PROJ_TN = 512
OUT_TM = 256
OUT_TK = 512
MOE_TM = 1024
MOE_TF = 256
MOE_TN = 1024
CMB_TQ = 128
ROUTER_TR = 2048


def _cparams(sem):
    return pltpu.CompilerParams(dimension_semantics=sem, vmem_limit_bytes=VMEM_LIMIT)


def _mod_kernel(c_ref, w_ref, b_ref, o_ref):
    c = c_ref[...]
    s = c * (1.0 / (1.0 + jnp.exp(-c)))
    acc = jnp.dot(s.astype(BF16), w_ref[...].astype(BF16), preferred_element_type=F32)
    o_ref[...] = acc + b_ref[...]


def _mod(c, w_mod, b_mod, tn=512):
    B, D = c.shape
    N = w_mod.shape[1]
    return pl.pallas_call(
        _mod_kernel,
        out_shape=jax.ShapeDtypeStruct((B, N), F32),
        grid=(N // tn,),
        in_specs=[pl.BlockSpec((B, D), lambda j: (0, 0)),
                  pl.BlockSpec((D, tn), lambda j: (0, j)),
                  pl.BlockSpec((1, tn), lambda j: (0, j))],
        out_specs=pl.BlockSpec((B, tn), lambda j: (0, j)),
        compiler_params=_cparams(("arbitrary",)),
        name="mod",
    )(c, w_mod, b_mod.reshape(1, N))


def _in_proj_kernel(x_ref, mod_ref, w_ref, b_ref, o_ref, u_ref):
    @pl.when(pl.program_id(2) == 0)
    def _():
        shift = mod_ref[0, 0:1, :]
        scale = mod_ref[0, 1:2, :]
        u_ref[...] = (x_ref[0] * (1.0 + scale) + shift).astype(BF16)

    acc = jnp.dot(u_ref[...], w_ref[...], preferred_element_type=F32)
    o_ref[0] = (acc + b_ref[...]).astype(BF16)


def _in_proj(x, mod, w_in_bf, b_in):
    B, S, D = x.shape
    N = w_in_bf.shape[1]
    tm, tn = min(PROJ_TM, S), PROJ_TN
    return pl.pallas_call(
        _in_proj_kernel,
        out_shape=jax.ShapeDtypeStruct((B, S, N), BF16),
        grid=(B, S // tm, N // tn),
        in_specs=[pl.BlockSpec((1, tm, D), lambda b, i, j: (b, i, 0)),
                  pl.BlockSpec((1, 6, D), lambda b, i, j: (b, 0, 0)),
                  pl.BlockSpec((D, tn), lambda b, i, j: (0, j)),
                  pl.BlockSpec((1, tn), lambda b, i, j: (0, j))],
        out_specs=pl.BlockSpec((1, tm, tn), lambda b, i, j: (b, i, j)),
        scratch_shapes=[pltpu.VMEM((tm, D), BF16)],
        compiler_params=_cparams(("arbitrary", "arbitrary", "arbitrary")),
        name="in_proj",
    )(x, mod, w_in_bf, b_in.reshape(1, N))


def _bias_kernel(table_ref, o_ref, *, col0, offset0, window):
    h = pl.program_id(0) + col0
    v = pl.program_id(1)
    rows, cols = o_ref.shape[2], o_ref.shape[3]
    qi = lax.broadcasted_iota(I32, (rows, cols), 0)
    kj = lax.broadcasted_iota(I32, (rows, cols), 1)
    dist = qi - kj + offset0 + v * rows
    n = jnp.maximum(dist, 0)
    max_exact = N_BUCKETS // 2
    nf = jnp.maximum(n, 1).astype(F32)
    large = max_exact + (jnp.log(nf / max_exact) / math.log(MAX_DISTANCE / max_exact)
                         * (N_BUCKETS - max_exact)).astype(I32)
    large = jnp.minimum(large, N_BUCKETS - 1)
    bucket = jnp.where(n < max_exact, n, large)
    val = jnp.zeros((rows, cols), F32)
    for bkt in range(N_BUCKETS):
        val = jnp.where(bucket == bkt, table_ref[bkt, h], val)
    valid = dist >= 0
    if window is not None:
        valid = valid & (dist < window)
    o_ref[0, 0] = jnp.where(valid, val, NEG_BIG)


def _bias_tiles(rel_bias, col0, n_heads, n_var, rows, cols, offset0, window):
    return pl.pallas_call(
        functools.partial(_bias_kernel, col0=col0, offset0=offset0, window=window),
        out_shape=jax.ShapeDtypeStruct((n_heads, n_var, rows, cols), F32),
        grid=(n_heads, n_var),
        in_specs=[pl.BlockSpec(memory_space=pltpu.SMEM)],
        out_specs=pl.BlockSpec((1, 1, rows, cols), lambda h, v: (h, v, 0, 0)),
        compiler_params=_cparams(("arbitrary", "arbitrary")),
        name="bias_tiles",
    )(rel_bias)


def _diff_attn_kernel(q_ref, k_ref, v_ref, bias_ref, lam_ref, g_ref, o_ref, m_ref, l_ref, acc_ref, *, lambda_init):
    t = q_ref.shape[1]
    d = HEAD_DIM_DIFF
    qi = pl.program_id(2)
    scale = d ** -0.5
    q = (q_ref[0].astype(F32) * scale).astype(BF16)
    q1, q2 = q[:, :d], q[:, d:]

    lv = lam_ref[...]
    lam = (jnp.exp(jnp.sum(lv[0:1] * lv[1:2], axis=-1, keepdims=True))
           - jnp.exp(jnp.sum(lv[2:3] * lv[3:4], axis=-1, keepdims=True)) + lambda_init)

    nt = (((1,), (1,)), ((), ()))
    m_ref[...] = jnp.full(m_ref.shape, NEG_BIG, F32)
    l_ref[...] = jnp.zeros(l_ref.shape, F32)
    acc_ref[...] = jnp.zeros(acc_ref.shape, F32)

    def step(kj, carry):
        k = k_ref[0, pl.ds(pl.multiple_of(kj * t, t), t), :]
        v = v_ref[0, pl.ds(pl.multiple_of(kj * t, t), t), :]
        bias = bias_ref[0, jnp.minimum(qi - kj, 2)]
        for c, qc in enumerate((q1, q2)):
            s = lax.dot_general(qc, k[:, c * d:(c + 1) * d], nt, preferred_element_type=F32) + bias
            m_old = m_ref[c]
            m_new = jnp.maximum(m_old, jnp.max(s, axis=-1, keepdims=True))
            alpha = jnp.exp(m_old - m_new)
            p = jnp.exp(s - m_new)
            l_ref[c] = alpha * l_ref[c] + jnp.sum(p, axis=-1, keepdims=True)
            acc_ref[c] = alpha * acc_ref[c] + jnp.dot(p.astype(BF16), v, preferred_element_type=F32)
            m_ref[c] = m_new
        return carry

    lax.fori_loop(0, qi + 1, step, 0)

    o = acc_ref[0] / l_ref[0] - lam * (acc_ref[1] / l_ref[1])
    o = o * lax.rsqrt(jnp.mean(jnp.square(o), axis=-1, keepdims=True) + RMS_EPS) * g_ref[...]
    o_ref[0] = (o * (1.0 - lambda_init)).astype(BF16)


def _diff_attn(proj, bias, lam_vecs, subln_g, n_heads, lambda_init):
    B, S, _ = proj.shape
    t = min(ATT_TILE, S)
    w = 2 * HEAD_DIM_DIFF
    return pl.pallas_call(
        functools.partial(_diff_attn_kernel, lambda_init=lambda_init),
        out_shape=jax.ShapeDtypeStruct((B, S, n_heads * w), BF16),
        grid=(B, n_heads, S // t),
        in_specs=[pl.BlockSpec((1, t, w), lambda b, h, i: (b, i, h)),
                  pl.BlockSpec((1, S, w), lambda b, h, i: (b, 0, n_heads + h)),
                  pl.BlockSpec((1, S, w), lambda b, h, i: (b, 0, 2 * n_heads + h)),
                  pl.BlockSpec((1, 3, t, t), lambda b, h, i: (h, 0, 0, 0)),
                  pl.BlockSpec((4, HEAD_DIM_DIFF), lambda b, h, i: (0, 0)),
                  pl.BlockSpec((1, w), lambda b, h, i: (0, 0))],
        out_specs=pl.BlockSpec((1, t, w), lambda b, h, i: (b, i, h)),
        scratch_shapes=[pltpu.VMEM((2, t, 1), F32), pltpu.VMEM((2, t, 1), F32), pltpu.VMEM((2, t, w), F32)],
        compiler_params=_cparams(("arbitrary", "arbitrary", "arbitrary")),
        name="diff_attn",
    )(proj, proj, proj, bias, lam_vecs, subln_g.reshape(1, w))


def _swa_kernel(sink_ref, q_ref, kc_ref, vc_ref, kp_ref, vp_ref, bias_ref, o_ref, *, n_kv):
    d = HEAD_DIM_SWA
    blk = pl.program_id(1)
    scale = d ** -0.5
    q = (q_ref[0].astype(F32) * scale).astype(BF16)
    kk = jnp.concatenate([kp_ref[0], kc_ref[0]], axis=0)
    vv = jnp.concatenate([vp_ref[0], vc_ref[0]], axis=0)
    rows = q.shape[0]
    kcol = lax.broadcasted_iota(I32, (rows, 2 * rows), 1)
    before_start = (blk == 0) & (kcol < rows)
    nt = (((1,), (1,)), ((), ()))
    for h in range(n_kv):
        k_h = kk[:, h * d:(h + 1) * d]
        v_h = vv[:, h * d:(h + 1) * d]
        for g in range(SWA_GROUP):
            hq = h * SWA_GROUP + g
            s = lax.dot_general(q[:, hq * d:(hq + 1) * d], k_h, nt, preferred_element_type=F32)
            s = jnp.where(before_start, NEG_BIG, s + bias_ref[hq])
            sink = sink_ref[hq]
            m = jnp.maximum(jnp.max(s, axis=-1, keepdims=True), sink)
            p = jnp.exp(s - m)
            denom = jnp.sum(p, axis=-1, keepdims=True) + jnp.exp(sink - m)
            o = jnp.dot(p.astype(BF16), v_h, preferred_element_type=F32) / denom
            o_ref[0, :, hq * d:(hq + 1) * d] = o.astype(BF16)


def _swa_attn(proj, bias, sinks, n_heads_diff, n_heads_swa):
    B, S, _ = proj.shape
    n_kv = n_heads_swa // SWA_GROUP
    r = WINDOW
    qw = n_heads_swa * HEAD_DIM_SWA
    kw = n_kv * HEAD_DIM_SWA
    q_blk = (3 * n_heads_diff * 2 * HEAD_DIM_DIFF) // qw
    k_blk = (3 * n_heads_diff * 2 * HEAD_DIM_DIFF + qw) // kw
    return pl.pallas_call(
        functools.partial(_swa_kernel, n_kv=n_kv),
        out_shape=jax.ShapeDtypeStruct((B, S, qw), BF16),
        grid=(B, S // r),
        in_specs=[pl.BlockSpec(memory_space=pltpu.SMEM),
                  pl.BlockSpec((1, r, qw), lambda b, i: (b, i, q_blk)),
                  pl.BlockSpec((1, r, kw), lambda b, i: (b, i, k_blk)),
                  pl.BlockSpec((1, r, kw), lambda b, i: (b, i, k_blk + 1)),
                  pl.BlockSpec((1, r, kw), lambda b, i: (b, jnp.maximum(i - 1, 0), k_blk)),
                  pl.BlockSpec((1, r, kw), lambda b, i: (b, jnp.maximum(i - 1, 0), k_blk + 1)),
                  pl.BlockSpec((n_heads_swa, r, 2 * r), lambda b, i: (0, 0, 0))],
        out_specs=pl.BlockSpec((1, r, qw), lambda b, i: (b, i, 0)),
        compiler_params=_cparams(("arbitrary", "arbitrary")),
        name="swa_attn",
    )(sinks, proj, proj, proj, proj, proj, bias)


def _pack_bf16_pair(lo, hi):
    lo_b = pltpu.bitcast(lo.astype(BF16).astype(F32), jnp.uint32) >> 16
    hi_b = pltpu.bitcast(hi.astype(BF16).astype(F32), jnp.uint32) & jnp.uint32(0xFFFF0000)
    return pltpu.bitcast(lo_b | hi_b, I32)


def _unpack_bf16_pair(w):
    wu = pltpu.bitcast(w, jnp.uint32)
    lo = pltpu.bitcast(wu << 16, F32).astype(BF16)
    hi = pltpu.bitcast(wu & jnp.uint32(0xFFFF0000), F32).astype(BF16)
    return lo, hi


def _out_proj_kernel(ad_ref, as_ref, w_ref, x_ref, mod_ref, bo_ref, g_ref, b_ref, wr_ref, br_ref,
                     x1_ref, u2_ref, lg_ref, acc_ref, *, n_kd):
    k = pl.program_id(2)
    nk = pl.num_programs(2)

    @pl.when(k == 0)
    def _():
        acc_ref[...] = jnp.zeros_like(acc_ref)

    @pl.when(k < n_kd)
    def _():
        acc_ref[...] += jnp.dot(ad_ref[0], w_ref[...], preferred_element_type=F32)

    @pl.when(k >= n_kd)
    def _():
        acc_ref[...] += jnp.dot(as_ref[0], w_ref[...], preferred_element_type=F32)

    @pl.when(k == nk - 1)
    def _():
        a = acc_ref[...] + bo_ref[...]
        gate1 = mod_ref[0, 2:3, :]
        shift2 = mod_ref[0, 3:4, :]
        scale2 = mod_ref[0, 4:5, :]
        h = DEEPNORM_ALPHA * x_ref[0] + (1.0 + gate1) * a
        mu = jnp.mean(h, axis=-1, keepdims=True)
        hc = h - mu
        var = jnp.mean(jnp.square(hc), axis=-1, keepdims=True)
        x1 = hc * lax.rsqrt(var + LN_EPS) * g_ref[...] + b_ref[...]
        x1_ref[0] = x1
        u2 = x1 * (1.0 + scale2) + shift2
        u2b = u2.astype(BF16)
        lg_ref[0] = jnp.dot(u2b, wr_ref[...], preferred_element_type=F32) + br_ref[...]
        half = u2.shape[1] // 2
        packed = _pack_bf16_pair(u2[:, :half], u2[:, half:])
        for c in range(half // LANES):
            u2_ref[0, :, c, :] = packed[:, c * LANES:(c + 1) * LANES]


def _out_proj(a_diff, a_swa, w_out_bf, x, mod, b_out, ln_g, ln_b, w_router_bf, b_router):
    B, S, D = x.shape
    E = w_router_bf.shape[1]
    tm, tk = min(OUT_TM, S), OUT_TK
    n_kd = a_diff.shape[2] // tk
    n_ks = a_swa.shape[2] // tk
    nc = D // 2 // LANES
    row = lambda b, i, k: (0, 0)
    return pl.pallas_call(
        functools.partial(_out_proj_kernel, n_kd=n_kd),
        out_shape=(jax.ShapeDtypeStruct((B, S, D), F32),
                   jax.ShapeDtypeStruct((B, S, nc, LANES), I32),
                   jax.ShapeDtypeStruct((B, S, E), F32)),
        grid=(B, S // tm, n_kd + n_ks),
        in_specs=[pl.BlockSpec((1, tm, tk), lambda b, i, k: (b, i, jnp.minimum(k, n_kd - 1))),
                  pl.BlockSpec((1, tm, tk), lambda b, i, k: (b, i, jnp.maximum(k - n_kd, 0))),
                  pl.BlockSpec((tk, D), lambda b, i, k: (k, 0)),
                  pl.BlockSpec((1, tm, D), lambda b, i, k: (b, i, 0)),
                  pl.BlockSpec((1, 6, D), lambda b, i, k: (b, 0, 0)),
                  pl.BlockSpec((1, D), row), pl.BlockSpec((1, D), row), pl.BlockSpec((1, D), row),
                  pl.BlockSpec((D, E), row), pl.BlockSpec((1, E), row)],
        out_specs=(pl.BlockSpec((1, tm, D), lambda b, i, k: (b, i, 0)),
                   pl.BlockSpec((1, tm, nc, LANES), lambda b, i, k: (b, i, 0, 0)),
                   pl.BlockSpec((1, tm, E), lambda b, i, k: (b, i, 0))),
        scratch_shapes=[pltpu.VMEM((tm, D), F32)],
        compiler_params=_cparams(("arbitrary", "arbitrary", "arbitrary")),
        name="out_proj",
    )(a_diff, a_swa, w_out_bf, x, mod, b_out.reshape(1, D), ln_g.reshape(1, D), ln_b.reshape(1, D),
      w_router_bf, b_router.reshape(1, E))


def _router_kernel(lg_ref, idx_ref, gate_ref):
    l = lg_ref[...]
    n, e = l.shape
    eidx = lax.broadcasted_iota(I32, (n, e), 1)
    kcol = lax.broadcasted_iota(I32, (n, TOP_K), 1)
    idx_out = jnp.zeros((n, TOP_K), I32)
    val_out = jnp.zeros((n, TOP_K), F32)
    for r in range(TOP_K):
        m = jnp.max(l, axis=-1, keepdims=True)
        i = jnp.min(jnp.where(l == m, eidx, e), axis=-1, keepdims=True)
        idx_out = jnp.where(kcol == r, i, idx_out)
        val_out = jnp.where(kcol == r, m, val_out)
        l = jnp.where(eidx == i, -jnp.inf, l)
    ex = jnp.exp(val_out - jnp.max(val_out, axis=-1, keepdims=True))
    idx_ref[...] = idx_out
    gate_ref[...] = ex / jnp.sum(ex, axis=-1, keepdims=True)


def _router(logits):
    T, E = logits.shape
    tr = min(ROUTER_TR, T)
    return pl.pallas_call(
        _router_kernel,
        out_shape=(jax.ShapeDtypeStruct((T, TOP_K), I32), jax.ShapeDtypeStruct((T, TOP_K), F32)),
        grid=(T // tr,),
        in_specs=[pl.BlockSpec((tr, E), lambda i: (i, 0))],
        out_specs=(pl.BlockSpec((tr, TOP_K), lambda i: (i, 0)), pl.BlockSpec((tr, TOP_K), lambda i: (i, 0))),
        compiler_params=_cparams(("arbitrary",)),
        name="router",
    )(logits)


def _routing_tables(top_idx, gates, n_experts, tm, n_tiles, tq):
    T = top_idx.shape[0]
    flat_e = top_idx.reshape(-1)
    onehot = (flat_e[:, None] == jnp.arange(n_experts, dtype=I32)[None, :]).astype(I32)
    csum = jnp.cumsum(onehot, axis=0)
    rank = jnp.sum(csum * onehot, axis=1) - 1
    counts = csum[-1]
    tiles_per = (counts + tm - 1) // tm
    tile_end = jnp.cumsum(tiles_per)
    tile_start = tile_end - tiles_per
    n_used = tile_end[-1]
    pos = tile_start[flat_e] * tm + rank
    rows = n_tiles * tm
    sorted_tok = jnp.zeros((rows,), I32).at[pos].set(jnp.arange(T * TOP_K, dtype=I32) // TOP_K)
    sorted_gate = jnp.zeros((rows,), F32).at[pos].set(gates.reshape(-1))
    tile_ids = jnp.arange(n_tiles, dtype=I32)
    tile_expert = jnp.sum((tile_ids[:, None] >= tile_end[None, :]).astype(I32), axis=1)
    tile_valid = (tile_ids < n_used).astype(I32)
    last_expert = jnp.max(jnp.where(counts > 0, jnp.arange(n_experts, dtype=I32), 0))
    tile_expert = jnp.where(tile_valid == 1, tile_expert, last_expert).astype(I32)
    pos_steps = pos.reshape(T // tq, tq, TOP_K).transpose(0, 2, 1).reshape(T // tq, TOP_K * tq)
    return (sorted_tok.reshape(n_tiles, tm), sorted_gate.reshape(rows, 1), tile_expert, tile_valid,
            pos_steps.astype(I32))


def _moe_up_kernel(te_ref, tv_ref, tok_hbm, u2_hbm, wg_ref, wl_ref, bg_ref, bl_ref, o_ref,
                   idx_smem, gbuf, xs_ref, wgb_ref, wlb_ref, idx_sem, row_sem):
    t = pl.program_id(0)
    j = pl.program_id(1)
    n_tiles = pl.num_programs(0)
    tm = xs_ref.shape[0]
    nc = gbuf.shape[1]
    valid = tv_ref[t] == 1

    def idx_copy(tile, slot):
        return pltpu.make_async_copy(tok_hbm.at[tile], idx_smem.at[slot], idx_sem.at[slot])

    def issue_rows(slot):
        def body(r, carry):
            tok = idx_smem[slot, r]
            pltpu.make_async_copy(u2_hbm.at[tok], gbuf.at[slot * tm + r], row_sem.at[slot]).start()
            return carry
        lax.fori_loop(0, tm, body, 0)

    def wait_rows(slot):
        pltpu.make_async_copy(u2_hbm.at[pl.ds(0, tm)], gbuf.at[pl.ds(slot * tm, tm)], row_sem.at[slot]).wait()

    @pl.when(j == 0)
    def _():
        slot = t % 2
        nxt = 1 - slot

        @pl.when(t == 0)
        def _():
            idx_copy(0, 0).start()
            idx_copy(0, 0).wait()

            @pl.when(tv_ref[0] == 1)
            def _():
                issue_rows(0)

            @pl.when(n_tiles > 1)
            def _():
                idx_copy(1, 1).start()

        @pl.when(t + 1 < n_tiles)
        def _():
            idx_copy(t + 1, nxt).wait()

            @pl.when(tv_ref[jnp.minimum(t + 1, n_tiles - 1)] == 1)
            def _():
                issue_rows(nxt)

        @pl.when(valid)
        def _():
            wait_rows(slot)
            half = nc * LANES
            for c in range(nc):
                lo, hi = _unpack_bf16_pair(gbuf[pl.ds(slot * tm, tm), c, :])
                xs_ref[:, c * LANES:(c + 1) * LANES] = lo
                xs_ref[:, half + c * LANES:half + (c + 1) * LANES] = hi

        @pl.when(t + 2 < n_tiles)
        def _():
            idx_copy(t + 2, slot).start()

    @pl.when(valid)
    def _():
        wgb_ref[...] = wg_ref[0].astype(BF16)
        wlb_ref[...] = wl_ref[0].astype(BF16)
        x = xs_ref[...]
        glu = jnp.dot(x, wgb_ref[...], preferred_element_type=F32) + bg_ref[0]
        lin = jnp.dot(x, wlb_ref[...], preferred_element_type=F32) + bl_ref[0]
        glu = jnp.minimum(glu, SWIGLU_LIMIT)
        lin = jnp.clip(lin, -SWIGLU_LIMIT, SWIGLU_LIMIT)
        act = glu * (1.0 / (1.0 + jnp.exp(-SWIGLU_ALPHA * glu))) * (lin + 1.0)
        o_ref[...] = act.astype(BF16)

    @pl.when(jnp.logical_not(valid))
    def _():
        o_ref[...] = jnp.zeros_like(o_ref)


def _moe_up(tile_expert, tile_valid, sorted_tok, u2p, w1, b1):
    E, D, F2 = w1.shape
    F = F2 // 2
    n_tiles, tm = sorted_tok.shape
    nc = u2p.shape[1]
    tf = min(MOE_TF, F)
    nj = F // tf

    def jj(t, j, te, tv):
        return jnp.where(tv[t] == 1, j, nj - 1)

    grid_spec = pltpu.PrefetchScalarGridSpec(
        num_scalar_prefetch=2,
        grid=(n_tiles, nj),
        in_specs=[pl.BlockSpec(memory_space=pl.ANY),
                  pl.BlockSpec(memory_space=pl.ANY),
                  pl.BlockSpec((1, D, tf), lambda t, j, te, tv: (te[t], 0, jj(t, j, te, tv))),
                  pl.BlockSpec((1, D, tf), lambda t, j, te, tv: (te[t], 0, nj + jj(t, j, te, tv))),
                  pl.BlockSpec((1, 1, tf), lambda t, j, te, tv: (te[t], 0, jj(t, j, te, tv))),
                  pl.BlockSpec((1, 1, tf), lambda t, j, te, tv: (te[t], 0, nj + jj(t, j, te, tv)))],
        out_specs=pl.BlockSpec((tm, tf), lambda t, j, te, tv: (t, j)),
        scratch_shapes=[pltpu.SMEM((2, tm), I32),
                        pltpu.VMEM((2 * tm, nc, LANES), I32),
                        pltpu.VMEM((tm, D), BF16),
                        pltpu.VMEM((D, tf), BF16),
                        pltpu.VMEM((D, tf), BF16),
                        pltpu.SemaphoreType.DMA((2,)),
                        pltpu.SemaphoreType.DMA((2,))],
    )
    return pl.pallas_call(
        _moe_up_kernel,
        out_shape=jax.ShapeDtypeStruct((n_tiles * tm, F), BF16),
        grid_spec=grid_spec,
        compiler_params=_cparams(("arbitrary", "arbitrary")),
        name="moe_up",
    )(tile_expert, tile_valid, sorted_tok, u2p, w1, w1, b1.reshape(E, 1, F2), b1.reshape(E, 1, F2))


def _moe_down_kernel(te_ref, tv_ref, a_ref, w_ref, b_ref, gate_ref, o_ref):
    t = pl.program_id(0)
    valid = tv_ref[t] == 1

    @pl.when(valid)
    def _():
        y = jnp.dot(a_ref[...], w_ref[0].astype(BF16), preferred_element_type=F32) + b_ref[0]
        y = y * gate_ref[...]
        for c in range(o_ref.shape[1]):
            o_ref[:, c, :] = y[:, c * LANES:(c + 1) * LANES]

    @pl.when(jnp.logical_not(valid))
    def _():
        o_ref[...] = jnp.zeros_like(o_ref)


def _moe_down(tile_expert, tile_valid, act, w2, b2, sorted_gate, tm):
    E, F, D = w2.shape
    rows = act.shape[0]
    n_tiles = rows // tm
    tn = min(MOE_TN, D)
    nj = D // tn

    def jj(t, j, te, tv):
        return jnp.where(tv[t] == 1, j, nj - 1)

    grid_spec = pltpu.PrefetchScalarGridSpec(
        num_scalar_prefetch=2,
        grid=(n_tiles, nj),
        in_specs=[pl.BlockSpec((tm, F), lambda t, j, te, tv: (t, 0)),
                  pl.BlockSpec((1, F, tn), lambda t, j, te, tv: (te[t], 0, jj(t, j, te, tv))),
                  pl.BlockSpec((1, 1, tn), lambda t, j, te, tv: (te[t], 0, jj(t, j, te, tv))),
                  pl.BlockSpec((tm, 1), lambda t, j, te, tv: (t, 0))],
        out_specs=pl.BlockSpec((tm, tn // LANES, LANES), lambda t, j, te, tv: (t, j, 0)),
    )
    return pl.pallas_call(
        _moe_down_kernel,
        out_shape=jax.ShapeDtypeStruct((rows, D // LANES, LANES), F32),
        grid_spec=grid_spec,
        compiler_params=_cparams(("arbitrary", "arbitrary")),
        name="moe_down",
    )(tile_expert, tile_valid, act, w2, b2.reshape(E, 1, D), sorted_gate)


def _combine_kernel(pos_hbm, y_hbm, x1_ref, mod_ref, g_ref, b_ref, o_ref, idx_smem, ybuf, idx_sem, row_sem):
    s = pl.program_id(0)
    n_steps = pl.num_programs(0)
    tq = x1_ref.shape[0]
    n_rows = TOP_K * tq
    nc = ybuf.shape[1]
    slot = s % 2
    nxt = 1 - slot

    def idx_copy(step, sl):
        return pltpu.make_async_copy(pos_hbm.at[step], idx_smem.at[sl], idx_sem.at[sl])

    def issue_rows(sl):
        def body(r, carry):
            row = idx_smem[sl, r]
            pltpu.make_async_copy(y_hbm.at[row], ybuf.at[sl * n_rows + r], row_sem.at[sl]).start()
            return carry
        lax.fori_loop(0, n_rows, body, 0)

    @pl.when(s == 0)
    def _():
        idx_copy(0, 0).start()
        idx_copy(0, 0).wait()
        issue_rows(0)

        @pl.when(n_steps > 1)
        def _():
            idx_copy(1, 1).start()

    @pl.when(s + 1 < n_steps)
    def _():
        idx_copy(s + 1, nxt).wait()
        issue_rows(nxt)

    pltpu.make_async_copy(y_hbm.at[pl.ds(0, n_rows)], ybuf.at[pl.ds(slot * n_rows, n_rows)],
                          row_sem.at[slot]).wait()

    @pl.when(s + 2 < n_steps)
    def _():
        idx_copy(s + 2, slot).start()

    chunks = []
    for c in range(nc):
        acc = ybuf[pl.ds(slot * n_rows, tq), c, :]
        for k in range(1, TOP_K):
            acc = acc + ybuf[pl.ds(slot * n_rows + k * tq, tq), c, :]
        chunks.append(acc)
    f = jnp.concatenate(chunks, axis=-1)
    gate2 = mod_ref[0, 5:6, :]
    h = DEEPNORM_ALPHA * x1_ref[...] + (1.0 + gate2) * f
    mu = jnp.mean(h, axis=-1, keepdims=True)
    hc = h - mu
    var = jnp.mean(jnp.square(hc), axis=-1, keepdims=True)
    o_ref[...] = hc * lax.rsqrt(var + LN_EPS) * g_ref[...] + b_ref[...]


def _combine(pos_steps, y, x1, mod, ln_g, ln_b, seq):
    T, D = x1.shape
    n_steps = pos_steps.shape[0]
    tq = T // n_steps
    nc = D // LANES
    steps_per_batch = seq // tq
    row = lambda s: (0, 0)
    return pl.pallas_call(
        _combine_kernel,
        out_shape=jax.ShapeDtypeStruct((T, D), F32),
        grid=(n_steps,),
        in_specs=[pl.BlockSpec(memory_space=pl.ANY),
                  pl.BlockSpec(memory_space=pl.ANY),
                  pl.BlockSpec((tq, D), lambda s: (s, 0)),
                  pl.BlockSpec((1, 6, D), lambda s: (s // steps_per_batch, 0, 0)),
                  pl.BlockSpec((1, D), row), pl.BlockSpec((1, D), row)],
        out_specs=pl.BlockSpec((tq, D), lambda s: (s, 0)),
        scratch_shapes=[pltpu.SMEM((2, TOP_K * tq), I32),
                        pltpu.VMEM((2 * TOP_K * tq, nc, LANES), F32),
                        pltpu.SemaphoreType.DMA((2,)),
                        pltpu.SemaphoreType.DMA((2,))],
        compiler_params=_cparams(("arbitrary",)),
        name="combine",
    )(pos_steps, y, x1, mod, ln_g.reshape(1, D), ln_b.reshape(1, D))


def kernel(x, c, rel_bias, w_mod, b_mod, w_in, b_in, lambda_q1, lambda_k1, lambda_q2, lambda_k2, subln_g, sinks,
           w_out, b_out, ln1_g, ln1_b, w_router, b_router, w1, b1, w2, b2, ln2_g, ln2_b):
    B, S, D = x.shape
    T = B * S
    E = w_router.shape[2]
    n_heads_diff = (D // 2) // (2 * HEAD_DIM_DIFF)
    n_heads_swa = (D // 2) // HEAD_DIM_SWA
    t_att = min(ATT_TILE, S)
    for l in range(DEPTH):
        lambda_init = 0.8 - 0.6 * math.exp(-0.3 * l)
        mod = _mod(c, w_mod[l], b_mod[l]).reshape(B, 6, D)
        proj = _in_proj(x, mod, w_in[l].astype(BF16), b_in[l])
        bias_d = _bias_tiles(rel_bias, 0, n_heads_diff, 3, t_att, t_att, 0, None)
        bias_s = _bias_tiles(rel_bias, n_heads_diff, n_heads_swa, 1, WINDOW, 2 * WINDOW, WINDOW, WINDOW)
        lam_vecs = jnp.stack([lambda_q1[l], lambda_k1[l], lambda_q2[l], lambda_k2[l]]).astype(F32)
        a_diff = _diff_attn(proj, bias_d, lam_vecs, subln_g[l], n_heads_diff, lambda_init)
        a_swa = _swa_attn(proj, bias_s.reshape(n_heads_swa, WINDOW, 2 * WINDOW), sinks[l], n_heads_diff, n_heads_swa)
        x1, u2p, logits = _out_proj(a_diff, a_swa, w_out[l].astype(BF16), x, mod, b_out[l], ln1_g[l], ln1_b[l],
                                    w_router[l].astype(BF16), b_router[l])
        top_idx, gates = _router(logits.reshape(T, E))
        tm = MOE_TM
        n_tiles = (T * TOP_K) // tm + E
        tq = min(CMB_TQ, S)
        sorted_tok, sorted_gate, tile_expert, tile_valid, pos_steps = _routing_tables(
            top_idx, gates, E, tm, n_tiles, tq)
        act = _moe_up(tile_expert, tile_valid, sorted_tok, u2p.reshape(T, D // 2 // LANES, LANES), w1[l], b1[l])
        y = _moe_down(tile_expert, tile_valid, act, w2[l], b2[l], sorted_gate, tm)
        x = _combine(pos_steps, y, x1.reshape(T, D), mod, ln2_g[l], ln2_b[l], S).reshape(B, S, D)
    return x
```

```python
import functools
import math

import jax
import jax.numpy as jnp
from jax import lax
from jax.experimental import pallas as pl
from jax.experimental.pallas import tpu as pltpu

F32 = jnp.float32
BF16 = jnp.bfloat16
I32 = jnp.int32

HEAD_DIM_DIFF = 128
HEAD_DIM_SWA = 64
SWA_GROUP = 8
WINDOW = 128
N_BUCKETS = 32
MAX_DISTANCE = 128
TOP_K = 4
SWIGLU_ALPHA = 1.702
SWIGLU_LIMIT = 7.0
LN_EPS = 1e-5
RMS_EPS = 1e-5
DEPTH = 1
DEEPNORM_ALPHA = (2.0 * DEPTH) ** 0.25
NEG_BIG = -1e30
LOG2E = 1.4426950408889634

LANES = 128
VMEM_LIMIT = 56 * 1024 * 1024

ATT_TILE = 512
ATT_SLAB = 32
PROJ_TM = 512
PROJ_TN = 512
OUT_TM = 256
OUT_TK = 512
MOE_TM = 1024
MOE_TF = 256
MOE_TN = 1024
UNPACK_ROWS = 256
DMA_UNROLL = 8
CMB_TQ = 128
ROUTER_TR = 2048


def _cparams(sem):
    return pltpu.CompilerParams(dimension_semantics=sem, vmem_limit_bytes=VMEM_LIMIT)


def _mod_kernel(c_ref, w_ref, b_ref, o_ref):
    c = c_ref[...]
    s = c * (1.0 / (1.0 + jnp.exp(-c)))
    acc = jnp.dot(s.astype(BF16), w_ref[...].astype(BF16), preferred_element_type=F32)
    o_ref[...] = acc + b_ref[...]


def _mod(c, w_mod, b_mod, tn=512):
    B, D = c.shape
    N = w_mod.shape[1]
    return pl.pallas_call(
        _mod_kernel,
        out_shape=jax.ShapeDtypeStruct((B, N), F32),
        grid=(N // tn,),
        in_specs=[pl.BlockSpec((B, D), lambda j: (0, 0)),
                  pl.BlockSpec((D, tn), lambda j: (0, j)),
                  pl.BlockSpec((1, tn), lambda j: (0, j))],
        out_specs=pl.BlockSpec((B, tn), lambda j: (0, j)),
        compiler_params=_cparams(("arbitrary",)),
        name="mod",
    )(c, w_mod, b_mod.reshape(1, N))


def _in_proj_kernel(x_ref, mod_ref, w_ref, b_ref, o_ref, u_ref):
    @pl.when(pl.program_id(2) == 0)
    def _():
        shift = mod_ref[0, 0:1, :]
        scale = mod_ref[0, 1:2, :]
        u_ref[...] = (x_ref[0] * (1.0 + scale) + shift).astype(BF16)

    acc = jnp.dot(u_ref[...], w_ref[...], preferred_element_type=F32)
    o_ref[0] = (acc + b_ref[...]).astype(BF16)


def _in_proj(x, mod, w_in_bf, b_in):
    B, S, D = x.shape
    N = w_in_bf.shape[1]
    tm, tn = min(PROJ_TM, S), PROJ_TN
    return pl.pallas_call(
        _in_proj_kernel,
        out_shape=jax.ShapeDtypeStruct((B, S, N), BF16),
        grid=(B, S // tm, N // tn),
        in_specs=[pl.BlockSpec((1, tm, D), lambda b, i, j: (b, i, 0)),
                  pl.BlockSpec((1, 6, D), lambda b, i, j: (b, 0, 0)),
                  pl.BlockSpec((D, tn), lambda b, i, j: (0, j)),
                  pl.BlockSpec((1, tn), lambda b, i, j: (0, j))],
        out_specs=pl.BlockSpec((1, tm, tn), lambda b, i, j: (b, i, j)),
        scratch_shapes=[pltpu.VMEM((tm, D), BF16)],
        compiler_params=_cparams(("arbitrary", "arbitrary", "arbitrary")),
        name="in_proj",
    )(x, mod, w_in_bf, b_in.reshape(1, N))


def _bias_kernel(table_ref, o_ref, *, col0, causal):
    h = pl.program_id(0) + col0
    v = pl.program_id(1)
    rows, cols = o_ref.shape[2], o_ref.shape[3]
    qi = lax.broadcasted_iota(I32, (rows, cols), 0)
    kj = lax.broadcasted_iota(I32, (rows, cols), 1)
    if causal:
        dist = qi - kj + v * rows
        valid = dist >= 0
    else:
        dist = qi - kj + rows
        valid = (dist >= 0) & (dist < WINDOW) & jnp.logical_not((v == 1) & (kj < rows))
    n = jnp.maximum(dist, 0)
    max_exact = N_BUCKETS // 2
    nf = jnp.maximum(n, 1).astype(F32)
    large = max_exact + (jnp.log(nf / max_exact) / math.log(MAX_DISTANCE / max_exact)
                         * (N_BUCKETS - max_exact)).astype(I32)
    large = jnp.minimum(large, N_BUCKETS - 1)
    bucket = jnp.where(n < max_exact, n, large)
    val = jnp.zeros((rows, cols), F32)
    for bkt in range(N_BUCKETS):
        val = jnp.where(bucket == bkt, table_ref[bkt, h], val)
    if causal:
        val = val - table_ref[N_BUCKETS - 1, h]
    o_ref[0, 0] = jnp.where(valid, val * LOG2E, NEG_BIG)


def _bias_tiles(rel_bias, col0, n_heads, n_var, rows, cols, causal):
    return pl.pallas_call(
        functools.partial(_bias_kernel, col0=col0, causal=causal),
        out_shape=jax.ShapeDtypeStruct((n_heads, n_var, rows, cols), F32),
        grid=(n_heads, n_var),
        in_specs=[pl.BlockSpec(memory_space=pltpu.SMEM)],
        out_specs=pl.BlockSpec((1, 1, rows, cols), lambda h, v: (h, v, 0, 0)),
        compiler_params=_cparams(("arbitrary", "arbitrary")),
        name="bias_tiles",
    )(rel_bias)


def _diff_attn_kernel(q_ref, k_ref, v_ref, bias_ref, lam_ref, g_ref, o_ref,
                      m_ref, l_ref, acc_ref, s_ref, p_ref, a_ref, *, lambda_init):
    t = q_ref.shape[1]
    d = HEAD_DIM_DIFF
    qi = pl.program_id(2)
    q = (q_ref[0].astype(F32) * (d ** -0.5 * LOG2E)).astype(BF16)

    lv = lam_ref[...]
    lam = (jnp.exp(jnp.sum(lv[0:1] * lv[1:2], axis=-1, keepdims=True))
           - jnp.exp(jnp.sum(lv[2:3] * lv[3:4], axis=-1, keepdims=True)) + lambda_init)

    nt = (((1,), (1,)), ((), ()))
    m_ref[...] = jnp.full(m_ref.shape, NEG_BIG, F32)
    l_ref[...] = jnp.zeros(l_ref.shape, F32)
    acc_ref[...] = jnp.zeros(acc_ref.shape, F32)

    def block(kj, var):
        k = k_ref[0, pl.ds(pl.multiple_of(kj * t, t), t), :]
        v = v_ref[0, pl.ds(pl.multiple_of(kj * t, t), t), :]
        for c in range(2):
            s_ref[c] = lax.dot_general(q[:, c * d:(c + 1) * d], k[:, c * d:(c + 1) * d], nt,
                                       preferred_element_type=F32)
        for c in range(2):
            for r0 in range(0, t, ATT_SLAB):
                rs = slice(r0, r0 + ATT_SLAB)
                ch = []
                for j0 in range(0, t, LANES):
                    sj = s_ref[c, rs, j0:j0 + LANES]
                    if var is not None:
                        sj = sj + bias_ref[0, var, rs, j0:j0 + LANES]
                    ch.append(sj)
                m_old = m_ref[c, rs, :]
                m_new = jnp.maximum(m_old, jnp.max(functools.reduce(jnp.maximum, ch), axis=-1, keepdims=True))
                alpha = jnp.exp2(m_old - m_new)
                ps = [jnp.exp2(sj - m_new) for sj in ch]
                l_ref[c, rs, :] = alpha * l_ref[c, rs, :] + functools.reduce(jnp.add, ps)
                m_ref[c, rs, :] = m_new
                a_ref[c, rs, :] = alpha
                for j, pj in enumerate(ps):
                    p_ref[c, rs, j * LANES:(j + 1) * LANES] = pj.astype(BF16)
            al = a_ref[c]
            acc_ref[c] = (acc_ref[c] * jnp.concatenate([al] * (2 * d // LANES), axis=-1)
                          + jnp.dot(p_ref[c], v, preferred_element_type=F32))

    def far_step(kj, carry):
        block(kj, None)
        return carry

    def near_step(kj, carry):
        block(kj, qi - kj)
        return carry

    n_far = jnp.maximum(qi - 1, 0)
    lax.fori_loop(0, n_far, far_step, 0)
    lax.fori_loop(n_far, qi + 1, near_step, 0)

    l0 = jnp.sum(l_ref[0], axis=-1, keepdims=True)
    l1 = jnp.sum(l_ref[1], axis=-1, keepdims=True)
    o = acc_ref[0] / l0 - lam * (acc_ref[1] / l1)
    o = o * lax.rsqrt(jnp.mean(jnp.square(o), axis=-1, keepdims=True) + RMS_EPS) * g_ref[...]
    o_ref[0] = (o * (1.0 - lambda_init)).astype(BF16)


def _diff_attn(proj, bias, lam_vecs, subln_g, n_heads, lambda_init):
    B, S, _ = proj.shape
    t = min(ATT_TILE, S)
    w = 2 * HEAD_DIM_DIFF
    return pl.pallas_call(
        functools.partial(_diff_attn_kernel, lambda_init=lambda_init),
        out_shape=jax.ShapeDtypeStruct((B, S, n_heads * w), BF16),
        grid=(B, n_heads, S // t),
        in_specs=[pl.BlockSpec((1, t, w), lambda b, h, i: (b, i, h)),
                  pl.BlockSpec((1, S, w), lambda b, h, i: (b, 0, n_heads + h)),
                  pl.BlockSpec((1, S, w), lambda b, h, i: (b, 0, 2 * n_heads + h)),
                  pl.BlockSpec((1, 2, t, t), lambda b, h, i: (h, 0, 0, 0)),
                  pl.BlockSpec((4, HEAD_DIM_DIFF), lambda b, h, i: (0, 0)),
                  pl.BlockSpec((1, w), lambda b, h, i: (0, 0))],
        out_specs=pl.BlockSpec((1, t, w), lambda b, h, i: (b, i, h)),
        scratch_shapes=[pltpu.VMEM((2, t, LANES), F32), pltpu.VMEM((2, t, LANES), F32), pltpu.VMEM((2, t, w), F32),
                        pltpu.VMEM((2, t, t), F32), pltpu.VMEM((2, t, t), BF16), pltpu.VMEM((2, t, LANES), F32)],
        compiler_params=_cparams(("arbitrary", "arbitrary", "arbitrary")),
        name="diff_attn",
    )(proj, proj, proj, bias, lam_vecs, subln_g.reshape(1, w))


def _swa_kernel(sink_ref, q_ref, kc_ref, vc_ref, kp_ref, vp_ref, bias_ref, o_ref, *, n_kv):
    d = HEAD_DIM_SWA
    var = jnp.where(pl.program_id(1) == 0, 1, 0)
    q = (q_ref[0].astype(F32) * (d ** -0.5 * LOG2E)).astype(BF16)
    kk = jnp.concatenate([kp_ref[0], kc_ref[0]], axis=0)
    vv = jnp.concatenate([vp_ref[0], vc_ref[0]], axis=0)
    lo_half = lax.broadcasted_iota(I32, (kk.shape[0], LANES), 1) < d
    zero = jnp.zeros((kk.shape[0], LANES), BF16)
    nt = (((1,), (1,)), ((), ()))

    def halves(x2, e):
        swapped = jnp.concatenate([x2[:, d:], x2[:, :d]], axis=1)
        own_lo = x2 if e == 0 else swapped
        own_hi = swapped if e == 0 else x2
        return jnp.where(lo_half, own_lo, zero), jnp.where(lo_half, zero, own_hi)

    for h in range(n_kv):
        pr, e = divmod(h, 2)
        k_lo, k_hi = halves(kk[:, pr * LANES:(pr + 1) * LANES], e)
        v_lo, v_hi = halves(vv[:, pr * LANES:(pr + 1) * LANES], e)
        heads = [h * SWA_GROUP + g for g in range(SWA_GROUP)]
        sinks = [sink_ref[hq] * LOG2E for hq in heads]
        s = [lax.dot_general(q[:, (hq // 2) * LANES:(hq // 2 + 1) * LANES], k_lo if hq % 2 == 0 else k_hi, nt,
                             preferred_element_type=F32) + bias_ref[hq, var] for hq in heads]
        m = [jnp.maximum(jnp.max(s_g, axis=-1, keepdims=True), sk) for s_g, sk in zip(s, sinks)]
        p = [jnp.exp2(s_g - m_g) for s_g, m_g in zip(s, m)]
        denom = [jnp.sum(p_g, axis=-1, keepdims=True) + jnp.exp2(sk - m_g) for p_g, m_g, sk in zip(p, m, sinks)]
        pn = [(p_g * (1.0 / d_g)).astype(BF16) for p_g, d_g in zip(p, denom)]
        for gp in range(SWA_GROUP // 2):
            col = (heads[2 * gp] // 2) * LANES
            o = (jnp.dot(pn[2 * gp], v_lo, preferred_element_type=F32)
                 + jnp.dot(pn[2 * gp + 1], v_hi, preferred_element_type=F32))
            o_ref[0, :, col:col + LANES] = o.astype(BF16)


def _swa_attn(proj, bias, sinks, n_heads_diff, n_heads_swa):
    B, S, _ = proj.shape
    n_kv = n_heads_swa // SWA_GROUP
    r = WINDOW
    qw = n_heads_swa * HEAD_DIM_SWA
    kw = n_kv * HEAD_DIM_SWA
    q_blk = (3 * n_heads_diff * 2 * HEAD_DIM_DIFF) // qw
    k_blk = (3 * n_heads_diff * 2 * HEAD_DIM_DIFF + qw) // kw
    return pl.pallas_call(
        functools.partial(_swa_kernel, n_kv=n_kv),
        out_shape=jax.ShapeDtypeStruct((B, S, qw), BF16),
        grid=(B, S // r),
        in_specs=[pl.BlockSpec(memory_space=pltpu.SMEM),
                  pl.BlockSpec((1, r, qw), lambda b, i: (b, i, q_blk)),
                  pl.BlockSpec((1, r, kw), lambda b, i: (b, i, k_blk)),
                  pl.BlockSpec((1, r, kw), lambda b, i: (b, i, k_blk + 1)),
                  pl.BlockSpec((1, r, kw), lambda b, i: (b, jnp.maximum(i - 1, 0), k_blk)),
                  pl.BlockSpec((1, r, kw), lambda b, i: (b, jnp.maximum(i - 1, 0), k_blk + 1)),
                  pl.BlockSpec((n_heads_swa, 2, r, 2 * r), lambda b, i: (0, 0, 0, 0))],
        out_specs=pl.BlockSpec((1, r, qw), lambda b, i: (b, i, 0)),
        compiler_params=_cparams(("arbitrary", "arbitrary")),
        name="swa_attn",
    )(sinks, proj, proj, proj, proj, proj, bias)


def _pack_bf16_pair(lo, hi):
    lo_b = pltpu.bitcast(lo.astype(BF16).astype(F32), jnp.uint32) >> 16
    hi_b = pltpu.bitcast(hi.astype(BF16).astype(F32), jnp.uint32) & jnp.uint32(0xFFFF0000)
    return pltpu.bitcast(lo_b | hi_b, I32)


def _unpack_bf16_pair(w):
    wu = pltpu.bitcast(w, jnp.uint32)
    lo = pltpu.bitcast(wu << 16, F32).astype(BF16)
    hi = pltpu.bitcast(wu & jnp.uint32(0xFFFF0000), F32).astype(BF16)
    return lo, hi


def _out_proj_kernel(ad_ref, as_ref, w_ref, x_ref, mod_ref, bo_ref, g_ref, b_ref, wr_ref, br_ref,
                     x1_ref, u2_ref, lg_ref, acc_ref, *, n_kd):
    k = pl.program_id(2)
    nk = pl.num_programs(2)

    @pl.when(k == 0)
    def _():
        acc_ref[...] = jnp.zeros_like(acc_ref)

    @pl.when(k < n_kd)
    def _():
        acc_ref[...] += jnp.dot(ad_ref[0], w_ref[...], preferred_element_type=F32)

    @pl.when(k >= n_kd)
    def _():
        acc_ref[...] += jnp.dot(as_ref[0], w_ref[...], preferred_element_type=F32)

    @pl.when(k == nk - 1)
    def _():
        a = acc_ref[...] + bo_ref[...]
        gate1 = mod_ref[0, 2:3, :]
        shift2 = mod_ref[0, 3:4, :]
        scale2 = mod_ref[0, 4:5, :]
        h = DEEPNORM_ALPHA * x_ref[0] + (1.0 + gate1) * a
        mu = jnp.mean(h, axis=-1, keepdims=True)
        hc = h - mu
        var = jnp.mean(jnp.square(hc), axis=-1, keepdims=True)
        x1 = hc * lax.rsqrt(var + LN_EPS) * g_ref[...] + b_ref[...]
        x1_ref[0] = x1
        u2 = x1 * (1.0 + scale2) + shift2
        u2b = u2.astype(BF16)
        lg_ref[0] = jnp.dot(u2b, wr_ref[...], preferred_element_type=F32) + br_ref[...]
        half = u2.shape[1] // 2
        u2_ref[0] = _pack_bf16_pair(u2[:, :half], u2[:, half:])


def _out_proj(a_diff, a_swa, w_out_bf, x, mod, b_out, ln_g, ln_b, w_router_bf, b_router):
    B, S, D = x.shape
    E = w_router_bf.shape[1]
    tm, tk = min(OUT_TM, S), OUT_TK
    n_kd = a_diff.shape[2] // tk
    n_ks = a_swa.shape[2] // tk
    row = lambda b, i, k: (0, 0)
    return pl.pallas_call(
        functools.partial(_out_proj_kernel, n_kd=n_kd),
        out_shape=(jax.ShapeDtypeStruct((B, S, D), F32),
                   jax.ShapeDtypeStruct((B, S, D // 2), I32),
                   jax.ShapeDtypeStruct((B, S, E), F32)),
        grid=(B, S // tm, n_kd + n_ks),
        in_specs=[pl.BlockSpec((1, tm, tk), lambda b, i, k: (b, i, jnp.minimum(k, n_kd - 1))),
                  pl.BlockSpec((1, tm, tk), lambda b, i, k: (b, i, jnp.maximum(k - n_kd, 0))),
                  pl.BlockSpec((tk, D), lambda b, i, k: (k, 0)),
                  pl.BlockSpec((1, tm, D), lambda b, i, k: (b, i, 0)),
                  pl.BlockSpec((1, 6, D), lambda b, i, k: (b, 0, 0)),
                  pl.BlockSpec((1, D), row), pl.BlockSpec((1, D), row), pl.BlockSpec((1, D), row),
                  pl.BlockSpec((D, E), row), pl.BlockSpec((1, E), row)],
        out_specs=(pl.BlockSpec((1, tm, D), lambda b, i, k: (b, i, 0)),
                   pl.BlockSpec((1, tm, D // 2), lambda b, i, k: (b, i, 0)),
                   pl.BlockSpec((1, tm, E), lambda b, i, k: (b, i, 0))),
        scratch_shapes=[pltpu.VMEM((tm, D), F32)],
        compiler_params=_cparams(("arbitrary", "arbitrary", "arbitrary")),
        name="out_proj",
    )(a_diff, a_swa, w_out_bf, x, mod, b_out.reshape(1, D), ln_g.reshape(1, D), ln_b.reshape(1, D),
      w_router_bf, b_router.reshape(1, E))


def _router_kernel(lg_ref, idx_ref, gate_ref):
    l = lg_ref[...]
    n, e = l.shape
    eidx = lax.broadcasted_iota(I32, (n, e), 1)
    kcol = lax.broadcasted_iota(I32, (n, TOP_K), 1)
    idx_out = jnp.zeros((n, TOP_K), I32)
    val_out = jnp.zeros((n, TOP_K), F32)
    for r in range(TOP_K):
        m = jnp.max(l, axis=-1, keepdims=True)
        i = jnp.min(jnp.where(l == m, eidx, e), axis=-1, keepdims=True)
        idx_out = jnp.where(kcol == r, i, idx_out)
        val_out = jnp.where(kcol == r, m, val_out)
        l = jnp.where(eidx == i, -jnp.inf, l)
    ex = jnp.exp(val_out - jnp.max(val_out, axis=-1, keepdims=True))
    idx_ref[...] = idx_out
    gate_ref[...] = ex / jnp.sum(ex, axis=-1, keepdims=True)


def _router(logits):
    T, E = logits.shape
    tr = min(ROUTER_TR, T)
    return pl.pallas_call(
        _router_kernel,
        out_shape=(jax.ShapeDtypeStruct((T, TOP_K), I32), jax.ShapeDtypeStruct((T, TOP_K), F32)),
        grid=(T // tr,),
        in_specs=[pl.BlockSpec((tr, E), lambda i: (i, 0))],
        out_specs=(pl.BlockSpec((tr, TOP_K), lambda i: (i, 0)), pl.BlockSpec((tr, TOP_K), lambda i: (i, 0))),
        compiler_params=_cparams(("arbitrary",)),
        name="router",
    )(logits)


def _routing_tables(top_idx, gates, n_experts, tm, n_tiles, tq):
    T = top_idx.shape[0]
    flat_e = top_idx.reshape(-1)
    onehot = (flat_e[:, None] == jnp.arange(n_experts, dtype=I32)[None, :]).astype(I32)
    csum = jnp.cumsum(onehot, axis=0)
    rank = jnp.sum(csum * onehot, axis=1) - 1
    counts = csum[-1]
    tiles_per = (counts + tm - 1) // tm
    tile_end = jnp.cumsum(tiles_per)
    tile_start = tile_end - tiles_per
    n_used = tile_end[-1]
    pos = tile_start[flat_e] * tm + rank
    rows = n_tiles * tm
    sorted_tok = jnp.zeros((rows,), I32).at[pos].set(jnp.arange(T * TOP_K, dtype=I32) // TOP_K)
    sorted_gate = jnp.zeros((rows,), F32).at[pos].set(gates.reshape(-1))
    tile_ids = jnp.arange(n_tiles, dtype=I32)
    tile_expert = jnp.sum((tile_ids[:, None] >= tile_end[None, :]).astype(I32), axis=1)
    tile_valid = (tile_ids < n_used).astype(I32)
    last_expert = jnp.max(jnp.where(counts > 0, jnp.arange(n_experts, dtype=I32), 0))
    tile_expert = jnp.where(tile_valid == 1, tile_expert, last_expert).astype(I32)
    pos_steps = pos.reshape(T // tq, tq, TOP_K).transpose(0, 2, 1).reshape(T // tq, TOP_K * tq)
    return (sorted_tok.reshape(n_tiles, tm), sorted_gate.reshape(rows, 1), tile_expert, tile_valid,
            pos_steps.astype(I32))


def _moe_up_kernel(te_ref, tv_ref, tok_hbm, u2_hbm, wg_ref, wl_ref, bg_ref, bl_ref, o_ref,
                   idx_smem, gbuf, xs_ref, wgb_ref, wlb_ref, idx_sem, row_sem):
    t = pl.program_id(0)
    j = pl.program_id(1)
    n_tiles = pl.num_programs(0)
    tm = xs_ref.shape[0]
    half = gbuf.shape[1]
    valid = tv_ref[t] == 1

    def idx_copy(tile, slot):
        return pltpu.make_async_copy(tok_hbm.at[tile], idx_smem.at[slot], idx_sem.at[slot])

    def issue_rows(slot):
        def body(r, carry):
            tok = idx_smem[slot, r]
            pltpu.make_async_copy(u2_hbm.at[pl.ds(tok, 1)], gbuf.at[pl.ds(slot * tm + r, 1)],
                                  row_sem.at[slot]).start()
            return carry
        lax.fori_loop(0, tm, body, 0, unroll=DMA_UNROLL)

    def wait_rows(slot):
        pltpu.make_async_copy(u2_hbm.at[pl.ds(0, tm)], gbuf.at[pl.ds(slot * tm, tm)], row_sem.at[slot]).wait()

    @pl.when(j == 0)
    def _():
        slot = t % 2
        nxt = 1 - slot

        @pl.when(t == 0)
        def _():
            idx_copy(0, 0).start()
            idx_copy(0, 0).wait()

            @pl.when(tv_ref[0] == 1)
            def _():
                issue_rows(0)

            @pl.when(n_tiles > 1)
            def _():
                idx_copy(1, 1).start()

        @pl.when(t + 1 < n_tiles)
        def _():
            idx_copy(t + 1, nxt).wait()

            @pl.when(tv_ref[jnp.minimum(t + 1, n_tiles - 1)] == 1)
            def _():
                issue_rows(nxt)

        @pl.when(valid)
        def _():
            wait_rows(slot)
            for r0 in range(0, tm, UNPACK_ROWS):
                lo, hi = _unpack_bf16_pair(gbuf[pl.ds(slot * tm + r0, UNPACK_ROWS), :])
                xs_ref[r0:r0 + UNPACK_ROWS, :half] = lo
                xs_ref[r0:r0 + UNPACK_ROWS, half:] = hi

        @pl.when(t + 2 < n_tiles)
        def _():
            idx_copy(t + 2, slot).start()

    @pl.when(valid)
    def _():
        wgb_ref[...] = wg_ref[0].astype(BF16)
        wlb_ref[...] = wl_ref[0].astype(BF16)
        x = xs_ref[...]
        glu = jnp.dot(x, wgb_ref[...], preferred_element_type=F32) + bg_ref[0]
        lin = jnp.dot(x, wlb_ref[...], preferred_element_type=F32) + bl_ref[0]
        glu = jnp.minimum(glu, SWIGLU_LIMIT)
        lin = jnp.clip(lin, -SWIGLU_LIMIT, SWIGLU_LIMIT)
        act = glu * (1.0 / (1.0 + jnp.exp(-SWIGLU_ALPHA * glu))) * (lin + 1.0)
        o_ref[...] = act.astype(BF16)

    @pl.when(jnp.logical_not(valid))
    def _():
        o_ref[...] = jnp.zeros_like(o_ref)


def _moe_up(tile_expert, tile_valid, sorted_tok, u2p, w1, b1):
    E, D, F2 = w1.shape
    F = F2 // 2
    n_tiles, tm = sorted_tok.shape
    tf = min(MOE_TF, F)
    nj = F // tf

    def jj(t, j, te, tv):
        return jnp.where(tv[t] == 1, j, nj - 1)

    grid_spec = pltpu.PrefetchScalarGridSpec(
        num_scalar_prefetch=2,
        grid=(n_tiles, nj),
        in_specs=[pl.BlockSpec(memory_space=pl.ANY),
                  pl.BlockSpec(memory_space=pl.ANY),
                  pl.BlockSpec((1, D, tf), lambda t, j, te, tv: (te[t], 0, jj(t, j, te, tv))),
                  pl.BlockSpec((1, D, tf), lambda t, j, te, tv: (te[t], 0, nj + jj(t, j, te, tv))),
                  pl.BlockSpec((1, 1, tf), lambda t, j, te, tv: (te[t], 0, jj(t, j, te, tv))),
                  pl.BlockSpec((1, 1, tf), lambda t, j, te, tv: (te[t], 0, nj + jj(t, j, te, tv)))],
        out_specs=pl.BlockSpec((tm, tf), lambda t, j, te, tv: (t, j)),
        scratch_shapes=[pltpu.SMEM((2, tm), I32),
                        pltpu.VMEM((2 * tm, D // 2), I32),
                        pltpu.VMEM((tm, D), BF16),
                        pltpu.VMEM((D, tf), BF16),
                        pltpu.VMEM((D, tf), BF16),
                        pltpu.SemaphoreType.DMA((2,)),
                        pltpu.SemaphoreType.DMA((2,))],
    )
    return pl.pallas_call(
        _moe_up_kernel,
        out_shape=jax.ShapeDtypeStruct((n_tiles * tm, F), BF16),
        grid_spec=grid_spec,
        compiler_params=_cparams(("arbitrary", "arbitrary")),
        name="moe_up",
    )(tile_expert, tile_valid, sorted_tok, u2p, w1, w1, b1.reshape(E, 1, F2), b1.reshape(E, 1, F2))


def _moe_down_kernel(te_ref, tv_ref, a_ref, w_ref, b_ref, gate_ref, o_ref):
    t = pl.program_id(0)
    valid = tv_ref[t] == 1

    @pl.when(valid)
    def _():
        y = jnp.dot(a_ref[...], w_ref[0].astype(BF16), preferred_element_type=F32) + b_ref[0]
        o_ref[...] = y * gate_ref[...]

    @pl.when(jnp.logical_not(valid))
    def _():
        o_ref[...] = jnp.zeros_like(o_ref)


def _moe_down(tile_expert, tile_valid, act, w2, b2, sorted_gate, tm):
    E, F, D = w2.shape
    rows = act.shape[0]
    n_tiles = rows // tm
    tn = min(MOE_TN, D)
    nj = D // tn

    def jj(t, j, te, tv):
        return jnp.where(tv[t] == 1, j, nj - 1)

    grid_spec = pltpu.PrefetchScalarGridSpec(
        num_scalar_prefetch=2,
        grid=(n_tiles, nj),
        in_specs=[pl.BlockSpec((tm, F), lambda t, j, te, tv: (t, 0)),
                  pl.BlockSpec((1, F, tn), lambda t, j, te, tv: (te[t], 0, jj(t, j, te, tv))),
                  pl.BlockSpec((1, 1, tn), lambda t, j, te, tv: (te[t], 0, jj(t, j, te, tv))),
                  pl.BlockSpec((tm, 1), lambda t, j, te, tv: (t, 0))],
        out_specs=pl.BlockSpec((tm, tn), lambda t, j, te, tv: (t, j)),
    )
    return pl.pallas_call(
        _moe_down_kernel,
        out_shape=jax.ShapeDtypeStruct((rows, D), F32),
        grid_spec=grid_spec,
        compiler_params=_cparams(("arbitrary", "arbitrary")),
        name="moe_down",
    )(tile_expert, tile_valid, act, w2, b2.reshape(E, 1, D), sorted_gate)


def _combine_kernel(pos_hbm, y_hbm, x1_ref, mod_ref, g_ref, b_ref, o_ref, idx_smem, ybuf, idx_sem, row_sem):
    s = pl.program_id(0)
    n_steps = pl.num_programs(0)
    tq = x1_ref.shape[0]
    n_rows = TOP_K * tq
    slot = s % 2
    nxt = 1 - slot

    def idx_copy(step, sl):
        return pltpu.make_async_copy(pos_hbm.at[step], idx_smem.at[sl], idx_sem.at[sl])

    def issue_rows(sl):
        def body(r, carry):
            row = idx_smem[sl, r]
            pltpu.make_async_copy(y_hbm.at[pl.ds(row, 1)], ybuf.at[pl.ds(sl * n_rows + r, 1)],
                                  row_sem.at[sl]).start()
            return carry
        lax.fori_loop(0, n_rows, body, 0, unroll=DMA_UNROLL)

    @pl.when(s == 0)
    def _():
        idx_copy(0, 0).start()
        idx_copy(0, 0).wait()
        issue_rows(0)

        @pl.when(n_steps > 1)
        def _():
            idx_copy(1, 1).start()

    @pl.when(s + 1 < n_steps)
    def _():
        idx_copy(s + 1, nxt).wait()
        issue_rows(nxt)

    pltpu.make_async_copy(y_hbm.at[pl.ds(0, n_rows)], ybuf.at[pl.ds(slot * n_rows, n_rows)],
                          row_sem.at[slot]).wait()

    @pl.when(s + 2 < n_steps)
    def _():
        idx_copy(s + 2, slot).start()

    f = ybuf[pl.ds(slot * n_rows, tq), :]
    for k in range(1, TOP_K):
        f = f + ybuf[pl.ds(slot * n_rows + k * tq, tq), :]
    gate2 = mod_ref[0, 5:6, :]
    h = DEEPNORM_ALPHA * x1_ref[...] + (1.0 + gate2) * f
    mu = jnp.mean(h, axis=-1, keepdims=True)
    hc = h - mu
    var = jnp.mean(jnp.square(hc), axis=-1, keepdims=True)
    o_ref[...] = hc * lax.rsqrt(var + LN_EPS) * g_ref[...] + b_ref[...]


def _combine(pos_steps, y, x1, mod, ln_g, ln_b, seq):
    T, D = x1.shape
    n_steps = pos_steps.shape[0]
    tq = T // n_steps
    steps_per_batch = seq // tq
    row = lambda s: (0, 0)
    return pl.pallas_call(
        _combine_kernel,
        out_shape=jax.ShapeDtypeStruct((T, D), F32),
        grid=(n_steps,),
        in_specs=[pl.BlockSpec(memory_space=pl.ANY),
                  pl.BlockSpec(memory_space=pl.ANY),
                  pl.BlockSpec((tq, D), lambda s: (s, 0)),
                  pl.BlockSpec((1, 6, D), lambda s: (s // steps_per_batch, 0, 0)),
                  pl.BlockSpec((1, D), row), pl.BlockSpec((1, D), row)],
        out_specs=pl.BlockSpec((tq, D), lambda s: (s, 0)),
        scratch_shapes=[pltpu.SMEM((2, TOP_K * tq), I32),
                        pltpu.VMEM((2 * TOP_K * tq, D), F32),
                        pltpu.SemaphoreType.DMA((2,)),
                        pltpu.SemaphoreType.DMA((2,))],
        compiler_params=_cparams(("arbitrary",)),
        name="combine",
    )(pos_steps, y, x1, mod, ln_g.reshape(1, D), ln_b.reshape(1, D))


def kernel(x, c, rel_bias, w_mod, b_mod, w_in, b_in, lambda_q1, lambda_k1, lambda_q2, lambda_k2, subln_g, sinks,
           w_out, b_out, ln1_g, ln1_b, w_router, b_router, w1, b1, w2, b2, ln2_g, ln2_b):
    B, S, D = x.shape
    T = B * S
    E = w_router.shape[2]
    n_heads_diff = (D // 2) // (2 * HEAD_DIM_DIFF)
    n_heads_swa = (D // 2) // HEAD_DIM_SWA
    t_att = min(ATT_TILE, S)
    for l in range(DEPTH):
        lambda_init = 0.8 - 0.6 * math.exp(-0.3 * l)
        mod = _mod(c, w_mod[l], b_mod[l]).reshape(B, 6, D)
        proj = _in_proj(x, mod, w_in[l].astype(BF16), b_in[l])
        bias_d = _bias_tiles(rel_bias, 0, n_heads_diff, 2, t_att, t_att, True)
        bias_s = _bias_tiles(rel_bias, n_heads_diff, n_heads_swa, 2, WINDOW, 2 * WINDOW, False)
        lam_vecs = jnp.stack([lambda_q1[l], lambda_k1[l], lambda_q2[l], lambda_k2[l]]).astype(F32)
        a_diff = _diff_attn(proj, bias_d, lam_vecs, subln_g[l], n_heads_diff, lambda_init)
        a_swa = _swa_attn(proj, bias_s, sinks[l], n_heads_diff, n_heads_swa)
        x1, u2p, logits = _out_proj(a_diff, a_swa, w_out[l].astype(BF16), x, mod, b_out[l], ln1_g[l], ln1_b[l],
                                    w_router[l].astype(BF16), b_router[l])
        top_idx, gates = _router(logits.reshape(T, E))
        tm = MOE_TM
        n_tiles = (T * TOP_K) // tm + E
        tq = min(CMB_TQ, S)
        sorted_tok, sorted_gate, tile_expert, tile_valid, pos_steps = _routing_tables(
            top_idx, gates, E, tm, n_tiles, tq)
        act = _moe_up(tile_expert, tile_valid, sorted_tok, u2p.reshape(T, D // 2), w1[l], b1[l])
        y = _moe_down(tile_expert, tile_valid, act, w2[l], b2[l], sorted_gate, tm)
        x = _combine(pos_steps, y, x1.reshape(T, D), mod, ln2_g[l], ln2_b[l], S).reshape(B, S, D)
    return x
```

```python
import functools
import math

import jax
import jax.numpy as jnp
from jax import lax
from jax.experimental import pallas as pl
from jax.experimental.pallas import tpu as pltpu

F32 = jnp.float32
BF16 = jnp.bfloat16
I32 = jnp.int32

HEAD_DIM_DIFF = 128
HEAD_DIM_SWA = 64
SWA_GROUP = 8
WINDOW = 128
N_BUCKETS = 32
MAX_DISTANCE = 128
TOP_K = 4
SWIGLU_ALPHA = 1.702
SWIGLU_LIMIT = 7.0
LN_EPS = 1e-5
RMS_EPS = 1e-5
DEPTH = 1
DEEPNORM_ALPHA = (2.0 * DEPTH) ** 0.25
NEG_BIG = -1e30
LOG2E = 1.4426950408889634

LANES = 128
VMEM_LIMIT = 56 * 1024 * 1024

ATT_TILE = 512
ATT_SLAB = 32
PROJ_TM = 1024
PROJ_TN = 512
OUT_TM = 512
OUT_TK = 512
ROW_CHUNK = 128
MOE_TM = 2048
MOE_SUB = 256
MOE_TF = 128
MOE_TN = 1024
DMA_UNROLL = 8
CMB_TQ = 256
ROUTER_TR = 2048


def _cparams(sem):
    return pltpu.CompilerParams(dimension_semantics=sem, vmem_limit_bytes=VMEM_LIMIT)


def _mod_kernel(c_ref, w_ref, b_ref, o_ref):
    c = c_ref[...]
    s = c * (1.0 / (1.0 + jnp.exp(-c)))
    acc = jnp.dot(s.astype(BF16), w_ref[...].astype(BF16), preferred_element_type=F32)
    o_ref[...] = acc + b_ref[...]


def _mod(c, w_mod, b_mod, tn=512):
    B, D = c.shape
    N = w_mod.shape[1]
    return pl.pallas_call(
        _mod_kernel,
        out_shape=jax.ShapeDtypeStruct((B, N), F32),
        grid=(N // tn,),
        in_specs=[pl.BlockSpec((B, D), lambda j: (0, 0)),
                  pl.BlockSpec((D, tn), lambda j: (0, j)),
                  pl.BlockSpec((1, tn), lambda j: (0, j))],
        out_specs=pl.BlockSpec((B, tn), lambda j: (0, j)),
        compiler_params=_cparams(("arbitrary",)),
        name="mod",
    )(c, w_mod, b_mod.reshape(1, N))


def _in_proj_kernel(x_hbm, mod_ref, w_ref, b_ref, o_ref, xbuf, u_ref, sem):
    i, j = pl.program_id(1), pl.program_id(2)
    ni = pl.num_programs(1)
    tm = xbuf.shape[0]
    step = pl.program_id(0) * ni + i
    n_steps = pl.num_programs(0) * ni

    def x_copy(s):
        return pltpu.make_async_copy(x_hbm.at[s // ni, pl.ds((s % ni) * tm, tm)], xbuf, sem)

    @pl.when(j == 0)
    def _():
        @pl.when(step == 0)
        def _():
            x_copy(0).start()

        x_copy(step).wait()
        shift = mod_ref[0, 0:1, :]
        scale = mod_ref[0, 1:2, :]
        for r0 in range(0, tm, ROW_CHUNK):
            u_ref[r0:r0 + ROW_CHUNK, :] = (xbuf[r0:r0 + ROW_CHUNK, :] * (1.0 + scale) + shift).astype(BF16)

    @pl.when((j == 1) & (step + 1 < n_steps))
    def _():
        x_copy(step + 1).start()

    acc = jnp.dot(u_ref[...], w_ref[...], preferred_element_type=F32)
    o_ref[0] = (acc + b_ref[...]).astype(BF16)


def _in_proj(x, mod, w_in_bf, b_in):
    B, S, D = x.shape
    N = w_in_bf.shape[1]
    tm, tn = min(PROJ_TM, S), PROJ_TN
    assert N // tn >= 2
    return pl.pallas_call(
        _in_proj_kernel,
        out_shape=jax.ShapeDtypeStruct((B, S, N), BF16),
        grid=(B, S // tm, N // tn),
        in_specs=[pl.BlockSpec(memory_space=pl.ANY),
                  pl.BlockSpec((1, 6, D), lambda b, i, j: (b, 0, 0)),
                  pl.BlockSpec((D, tn), lambda b, i, j: (0, j)),
                  pl.BlockSpec((1, tn), lambda b, i, j: (0, j))],
        out_specs=pl.BlockSpec((1, tm, tn), lambda b, i, j: (b, i, j)),
        scratch_shapes=[pltpu.VMEM((tm, D), F32), pltpu.VMEM((tm, D), BF16), pltpu.SemaphoreType.DMA(())],
        compiler_params=_cparams(("arbitrary", "arbitrary", "arbitrary")),
        name="in_proj",
    )(x, mod, w_in_bf, b_in.reshape(1, N))


def _bias_kernel(table_ref, o_ref, *, col0, causal):
    h = pl.program_id(0) + col0
    v = pl.program_id(1)
    rows, cols = o_ref.shape[2], o_ref.shape[3]
    qi = lax.broadcasted_iota(I32, (rows, cols), 0)
    kj = lax.broadcasted_iota(I32, (rows, cols), 1)
    if causal:
        dist = qi - kj + v * rows
        valid = dist >= 0
    else:
        dist = qi - kj + rows
        valid = (dist >= 0) & (dist < WINDOW) & jnp.logical_not((v == 1) & (kj < rows))
    n = jnp.maximum(dist, 0)
    max_exact = N_BUCKETS // 2
    nf = jnp.maximum(n, 1).astype(F32)
    large = max_exact + (jnp.log(nf / max_exact) / math.log(MAX_DISTANCE / max_exact)
                         * (N_BUCKETS - max_exact)).astype(I32)
    large = jnp.minimum(large, N_BUCKETS - 1)
    bucket = jnp.where(n < max_exact, n, large)
    val = jnp.zeros((rows, cols), F32)
    for bkt in range(N_BUCKETS):
        val = jnp.where(bucket == bkt, table_ref[bkt, h], val)
    if causal:
        val = val - table_ref[N_BUCKETS - 1, h]
    o_ref[0, 0] = jnp.where(valid, val * LOG2E, NEG_BIG)


def _bias_tiles(rel_bias, col0, n_heads, n_var, rows, cols, causal):
    return pl.pallas_call(
        functools.partial(_bias_kernel, col0=col0, causal=causal),
        out_shape=jax.ShapeDtypeStruct((n_heads, n_var, rows, cols), F32),
        grid=(n_heads, n_var),
        in_specs=[pl.BlockSpec(memory_space=pltpu.SMEM)],
        out_specs=pl.BlockSpec((1, 1, rows, cols), lambda h, v: (h, v, 0, 0)),
        compiler_params=_cparams(("arbitrary", "arbitrary")),
        name="bias_tiles",
    )(rel_bias)


def _diff_attn_kernel(q_ref, k_ref, v_ref, bias_ref, lam_ref, g_ref, o_ref,
                      m_ref, l_ref, acc_ref, s_ref, p_ref, a_ref, *, lambda_init):
    t = q_ref.shape[1]
    d = HEAD_DIM_DIFF
    qi = pl.program_id(2)
    q = (q_ref[0].astype(F32) * (d ** -0.5 * LOG2E)).astype(BF16)

    lv = lam_ref[...]
    lam = (jnp.exp(jnp.sum(lv[0:1] * lv[1:2], axis=-1, keepdims=True))
           - jnp.exp(jnp.sum(lv[2:3] * lv[3:4], axis=-1, keepdims=True)) + lambda_init)

    nt = (((1,), (1,)), ((), ()))
    m_ref[...] = jnp.full(m_ref.shape, NEG_BIG, F32)
    l_ref[...] = jnp.zeros(l_ref.shape, F32)
    acc_ref[...] = jnp.zeros(acc_ref.shape, F32)

    def block(kj, var):
        k = k_ref[0, pl.ds(pl.multiple_of(kj * t, t), t), :]
        v = v_ref[0, pl.ds(pl.multiple_of(kj * t, t), t), :]
        for c in range(2):
            s_ref[c] = lax.dot_general(q[:, c * d:(c + 1) * d], k[:, c * d:(c + 1) * d], nt,
                                       preferred_element_type=F32)
        for c in range(2):
            for r0 in range(0, t, ATT_SLAB):
                rs = slice(r0, r0 + ATT_SLAB)
                ch = []
                for j0 in range(0, t, LANES):
                    sj = s_ref[c, rs, j0:j0 + LANES]
                    if var is not None:
                        sj = sj + bias_ref[0, var, rs, j0:j0 + LANES]
                    ch.append(sj)
                m_old = m_ref[c, rs, :]
                m_new = jnp.maximum(m_old, jnp.max(functools.reduce(jnp.maximum, ch), axis=-1, keepdims=True))
                alpha = jnp.exp2(m_old - m_new)
                ps = [jnp.exp2(sj - m_new) for sj in ch]
                l_ref[c, rs, :] = alpha * l_ref[c, rs, :] + functools.reduce(jnp.add, ps)
                m_ref[c, rs, :] = m_new
                a_ref[c, rs, :] = alpha
                for j, pj in enumerate(ps):
                    p_ref[c, rs, j * LANES:(j + 1) * LANES] = pj.astype(BF16)
            al = a_ref[c]
            acc_ref[c] = (acc_ref[c] * jnp.concatenate([al] * (2 * d // LANES), axis=-1)
                          + jnp.dot(p_ref[c], v, preferred_element_type=F32))

    def far_step(kj, carry):
        block(kj, None)
        return carry

    def near_step(kj, carry):
        block(kj, qi - kj)
        return carry

    n_far = jnp.maximum(qi - 1, 0)
    lax.fori_loop(0, n_far, far_step, 0)
    lax.fori_loop(n_far, qi + 1, near_step, 0)

    l0 = jnp.sum(l_ref[0], axis=-1, keepdims=True)
    l1 = jnp.sum(l_ref[1], axis=-1, keepdims=True)
    o = acc_ref[0] / l0 - lam * (acc_ref[1] / l1)
    o = o * lax.rsqrt(jnp.mean(jnp.square(o), axis=-1, keepdims=True) + RMS_EPS) * g_ref[...]
    o_ref[0] = (o * (1.0 - lambda_init)).astype(BF16)


def _diff_attn(proj, bias, lam_vecs, subln_g, n_heads, lambda_init):
    B, S, _ = proj.shape
    t = min(ATT_TILE, S)
    w = 2 * HEAD_DIM_DIFF
    return pl.pallas_call(
        functools.partial(_diff_attn_kernel, lambda_init=lambda_init),
        out_shape=jax.ShapeDtypeStruct((B, S, n_heads * w), BF16),
        grid=(B, n_heads, S // t),
        in_specs=[pl.BlockSpec((1, t, w), lambda b, h, i: (b, i, h)),
                  pl.BlockSpec((1, S, w), lambda b, h, i: (b, 0, n_heads + h)),
                  pl.BlockSpec((1, S, w), lambda b, h, i: (b, 0, 2 * n_heads + h)),
                  pl.BlockSpec((1, 2, t, t), lambda b, h, i: (h, 0, 0, 0)),
                  pl.BlockSpec((4, HEAD_DIM_DIFF), lambda b, h, i: (0, 0)),
                  pl.BlockSpec((1, w), lambda b, h, i: (0, 0))],
        out_specs=pl.BlockSpec((1, t, w), lambda b, h, i: (b, i, h)),
        scratch_shapes=[pltpu.VMEM((2, t, LANES), F32), pltpu.VMEM((2, t, LANES), F32), pltpu.VMEM((2, t, w), F32),
                        pltpu.VMEM((2, t, t), F32), pltpu.VMEM((2, t, t), BF16), pltpu.VMEM((2, t, LANES), F32)],
        compiler_params=_cparams(("arbitrary", "arbitrary", "arbitrary")),
        name="diff_attn",
    )(proj, proj, proj, bias, lam_vecs, subln_g.reshape(1, w))


def _swa_kernel(sink_ref, q_ref, kc_ref, vc_ref, kp_ref, vp_ref, bias_ref, o_ref, *, n_kv):
    d = HEAD_DIM_SWA
    var = jnp.where(pl.program_id(1) == 0, 1, 0)
    q = (q_ref[0].astype(F32) * (d ** -0.5 * LOG2E)).astype(BF16)
    kk = jnp.concatenate([kp_ref[0], kc_ref[0]], axis=0)
    vv = jnp.concatenate([vp_ref[0], vc_ref[0]], axis=0)
    lo_half = lax.broadcasted_iota(I32, (kk.shape[0], LANES), 1) < d
    zero = jnp.zeros((kk.shape[0], LANES), BF16)
    nt = (((1,), (1,)), ((), ()))

    def halves(x2, e):
        swapped = jnp.concatenate([x2[:, d:], x2[:, :d]], axis=1)
        own_lo = x2 if e == 0 else swapped
        own_hi = swapped if e == 0 else x2
        return jnp.where(lo_half, own_lo, zero), jnp.where(lo_half, zero, own_hi)

    for h in range(n_kv):
        pr, e = divmod(h, 2)
        k_lo, k_hi = halves(kk[:, pr * LANES:(pr + 1) * LANES], e)
        v_lo, v_hi = halves(vv[:, pr * LANES:(pr + 1) * LANES], e)
        heads = [h * SWA_GROUP + g for g in range(SWA_GROUP)]
        sinks = [sink_ref[hq] * LOG2E for hq in heads]
        s = [lax.dot_general(q[:, (hq // 2) * LANES:(hq // 2 + 1) * LANES], k_lo if hq % 2 == 0 else k_hi, nt,
                             preferred_element_type=F32) + bias_ref[hq, var] for hq in heads]
        m = [jnp.maximum(jnp.max(s_g, axis=-1, keepdims=True), sk) for s_g, sk in zip(s, sinks)]
        p = [jnp.exp2(s_g - m_g) for s_g, m_g in zip(s, m)]
        denom = [jnp.sum(p_g, axis=-1, keepdims=True) + jnp.exp2(sk - m_g) for p_g, m_g, sk in zip(p, m, sinks)]
        pn = [(p_g * (1.0 / d_g)).astype(BF16) for p_g, d_g in zip(p, denom)]
        for gp in range(SWA_GROUP // 2):
            col = (heads[2 * gp] // 2) * LANES
            o = (jnp.dot(pn[2 * gp], v_lo, preferred_element_type=F32)
                 + jnp.dot(pn[2 * gp + 1], v_hi, preferred_element_type=F32))
            o_ref[0, :, col:col + LANES] = o.astype(BF16)


def _swa_attn(proj, bias, sinks, n_heads_diff, n_heads_swa):
    B, S, _ = proj.shape
    n_kv = n_heads_swa // SWA_GROUP
    r = WINDOW
    qw = n_heads_swa * HEAD_DIM_SWA
    kw = n_kv * HEAD_DIM_SWA
    q_blk = (3 * n_heads_diff * 2 * HEAD_DIM_DIFF) // qw
    k_blk = (3 * n_heads_diff * 2 * HEAD_DIM_DIFF + qw) // kw
    return pl.pallas_call(
        functools.partial(_swa_kernel, n_kv=n_kv),
        out_shape=jax.ShapeDtypeStruct((B, S, qw), BF16),
        grid=(B, S // r),
        in_specs=[pl.BlockSpec(memory_space=pltpu.SMEM),
                  pl.BlockSpec((1, r, qw), lambda b, i: (b, i, q_blk)),
                  pl.BlockSpec((1, r, kw), lambda b, i: (b, i, k_blk)),
                  pl.BlockSpec((1, r, kw), lambda b, i: (b, i, k_blk + 1)),
                  pl.BlockSpec((1, r, kw), lambda b, i: (b, jnp.maximum(i - 1, 0), k_blk)),
                  pl.BlockSpec((1, r, kw), lambda b, i: (b, jnp.maximum(i - 1, 0), k_blk + 1)),
                  pl.BlockSpec((n_heads_swa, 2, r, 2 * r), lambda b, i: (0, 0, 0, 0))],
        out_specs=pl.BlockSpec((1, r, qw), lambda b, i: (b, i, 0)),
        compiler_params=_cparams(("arbitrary", "arbitrary")),
        name="swa_attn",
    )(sinks, proj, proj, proj, proj, proj, bias)


def _pack_bf16_pair(lo, hi):
    lo_b = pltpu.bitcast(lo.astype(BF16).astype(F32), jnp.uint32) >> 16
    hi_b = pltpu.bitcast(hi.astype(BF16).astype(F32), jnp.uint32) & jnp.uint32(0xFFFF0000)
    return pltpu.bitcast(lo_b | hi_b, I32)


def _unpack_bf16_pair(w):
    wu = pltpu.bitcast(w, jnp.uint32)
    lo = pltpu.bitcast(wu << 16, F32).astype(BF16)
    hi = pltpu.bitcast(wu & jnp.uint32(0xFFFF0000), F32).astype(BF16)
    return lo, hi


def _out_proj_kernel(ad_ref, as_ref, w_ref, x_hbm, mod_ref, bo_ref, g_ref, b_ref, wr_ref, br_ref,
                     x1_hbm, u2_ref, lg_ref, acc_ref, xbuf, x1buf, x_sem, o_sem, *, n_kd):
    i, k = pl.program_id(1), pl.program_id(2)
    ni, nk = pl.num_programs(1), pl.num_programs(2)
    tm = xbuf.shape[0]
    step = pl.program_id(0) * ni + i
    n_steps = pl.num_programs(0) * ni

    def x_copy(s):
        return pltpu.make_async_copy(x_hbm.at[s // ni, pl.ds((s % ni) * tm, tm)], xbuf, x_sem)

    def o_copy(s):
        return pltpu.make_async_copy(x1buf, x1_hbm.at[s // ni, pl.ds((s % ni) * tm, tm)], o_sem)

    @pl.when(k == 0)
    def _():
        x_copy(step).start()
        acc_ref[...] = jnp.dot(ad_ref[0], w_ref[...], preferred_element_type=F32)

    @pl.when((k > 0) & (k < n_kd))
    def _():
        acc_ref[...] += jnp.dot(ad_ref[0], w_ref[...], preferred_element_type=F32)

    @pl.when(k >= n_kd)
    def _():
        acc_ref[...] += jnp.dot(as_ref[0], w_ref[...], preferred_element_type=F32)

    @pl.when(k == nk - 1)
    def _():
        x_copy(step).wait()

        @pl.when(step > 0)
        def _():
            o_copy(step - 1).wait()

        gate1 = mod_ref[0, 2:3, :]
        shift2 = mod_ref[0, 3:4, :]
        scale2 = mod_ref[0, 4:5, :]
        half = xbuf.shape[1] // 2
        for r0 in range(0, tm, ROW_CHUNK):
            rs = slice(r0, r0 + ROW_CHUNK)
            a = acc_ref[rs, :] + bo_ref[...]
            h = DEEPNORM_ALPHA * xbuf[rs, :] + (1.0 + gate1) * a
            mu = jnp.mean(h, axis=-1, keepdims=True)
            hc = h - mu
            var = jnp.mean(jnp.square(hc), axis=-1, keepdims=True)
            x1 = hc * lax.rsqrt(var + LN_EPS) * g_ref[...] + b_ref[...]
            x1buf[rs, :] = x1
            u2 = x1 * (1.0 + scale2) + shift2
            lg_ref[0, rs, :] = jnp.dot(u2.astype(BF16), wr_ref[...], preferred_element_type=F32) + br_ref[...]
            u2_ref[0, rs, :] = _pack_bf16_pair(u2[:, :half], u2[:, half:])
        o_copy(step).start()

        @pl.when(step == n_steps - 1)
        def _():
            o_copy(step).wait()


def _out_proj(a_diff, a_swa, w_out_bf, x, mod, b_out, ln_g, ln_b, w_router_bf, b_router):
    B, S, D = x.shape
    E = w_router_bf.shape[1]
    tm, tk = min(OUT_TM, S), OUT_TK
    n_kd = a_diff.shape[2] // tk
    n_ks = a_swa.shape[2] // tk
    row = lambda b, i, k: (0, 0)
    return pl.pallas_call(
        functools.partial(_out_proj_kernel, n_kd=n_kd),
        out_shape=(jax.ShapeDtypeStruct((B, S, D), F32),
                   jax.ShapeDtypeStruct((B, S, D // 2), I32),
                   jax.ShapeDtypeStruct((B, S, E), F32)),
        grid=(B, S // tm, n_kd + n_ks),
        in_specs=[pl.BlockSpec((1, tm, tk), lambda b, i, k: (b, i, jnp.minimum(k, n_kd - 1))),
                  pl.BlockSpec((1, tm, tk), lambda b, i, k: (b, i, jnp.maximum(k - n_kd, 0))),
                  pl.BlockSpec((tk, D), lambda b, i, k: (k, 0)),
                  pl.BlockSpec(memory_space=pl.ANY),
                  pl.BlockSpec((1, 6, D), lambda b, i, k: (b, 0, 0)),
                  pl.BlockSpec((1, D), row), pl.BlockSpec((1, D), row), pl.BlockSpec((1, D), row),
                  pl.BlockSpec((D, E), row), pl.BlockSpec((1, E), row)],
        out_specs=(pl.BlockSpec(memory_space=pl.ANY),
                   pl.BlockSpec((1, tm, D // 2), lambda b, i, k: (b, i, 0)),
                   pl.BlockSpec((1, tm, E), lambda b, i, k: (b, i, 0))),
        scratch_shapes=[pltpu.VMEM((tm, D), F32), pltpu.VMEM((tm, D), F32), pltpu.VMEM((tm, D), F32),
                        pltpu.SemaphoreType.DMA(()), pltpu.SemaphoreType.DMA(())],
        compiler_params=_cparams(("arbitrary", "arbitrary", "arbitrary")),
        name="out_proj",
    )(a_diff, a_swa, w_out_bf, x, mod, b_out.reshape(1, D), ln_g.reshape(1, D), ln_b.reshape(1, D),
      w_router_bf, b_router.reshape(1, E))


def _router_kernel(lg_ref, idx_ref, gate_ref):
    l = lg_ref[...]
    n, e = l.shape
    eidx = lax.broadcasted_iota(I32, (n, e), 1)
    kcol = lax.broadcasted_iota(I32, (n, TOP_K), 1)
    idx_out = jnp.zeros((n, TOP_K), I32)
    val_out = jnp.zeros((n, TOP_K), F32)
    for r in range(TOP_K):
        m = jnp.max(l, axis=-1, keepdims=True)
        i = jnp.min(jnp.where(l == m, eidx, e), axis=-1, keepdims=True)
        idx_out = jnp.where(kcol == r, i, idx_out)
        val_out = jnp.where(kcol == r, m, val_out)
        l = jnp.where(eidx == i, -jnp.inf, l)
    ex = jnp.exp(val_out - jnp.max(val_out, axis=-1, keepdims=True))
    idx_ref[...] = idx_out
    gate_ref[...] = ex / jnp.sum(ex, axis=-1, keepdims=True)


def _router(logits):
    T, E = logits.shape
    tr = min(ROUTER_TR, T)
    return pl.pallas_call(
        _router_kernel,
        out_shape=(jax.ShapeDtypeStruct((T, TOP_K), I32), jax.ShapeDtypeStruct((T, TOP_K), F32)),
        grid=(T // tr,),
        in_specs=[pl.BlockSpec((tr, E), lambda i: (i, 0))],
        out_specs=(pl.BlockSpec((tr, TOP_K), lambda i: (i, 0)), pl.BlockSpec((tr, TOP_K), lambda i: (i, 0))),
        compiler_params=_cparams(("arbitrary",)),
        name="router",
    )(logits)


def _routing_tables(top_idx, n_experts, tm, sub, n_tiles, tq):
    T = top_idx.shape[0]
    flat_e = top_idx.reshape(-1)
    onehot = (flat_e[:, None] == jnp.arange(n_experts, dtype=I32)[None, :]).astype(I32)
    csum = jnp.cumsum(onehot, axis=0)
    rank = jnp.sum(csum * onehot, axis=1) - 1
    counts = csum[-1]
    tiles_per = (counts + tm - 1) // tm
    tile_end = jnp.cumsum(tiles_per)
    tile_start = tile_end - tiles_per
    n_used = tile_end[-1]
    pos = tile_start[flat_e] * tm + rank
    tile_ids = jnp.arange(n_tiles, dtype=I32)
    tile_expert = jnp.sum((tile_ids[:, None] >= tile_end[None, :]).astype(I32), axis=1)
    last_expert = jnp.max(jnp.where(counts > 0, jnp.arange(n_experts, dtype=I32), 0))
    tile_expert = jnp.where(tile_ids < n_used, tile_expert, last_expert).astype(I32)
    first_in_expert = (tile_ids - tile_start[tile_expert]) * tm
    tile_rows = jnp.where(tile_ids < n_used, jnp.clip(counts[tile_expert] - first_in_expert, 0, tm), 0)
    tile_sub = ((tile_rows + sub - 1) // sub).astype(I32)
    first_pair = (jnp.cumsum(counts) - counts)[tile_expert] + first_in_expert
    order = jnp.argsort(flat_e, stable=True).astype(I32)
    order = jnp.concatenate([order // TOP_K, jnp.zeros((tm,), I32)])
    sorted_tok = jax.vmap(lambda s: lax.dynamic_slice(order, (s,), (tm,)))(jnp.clip(first_pair, 0, T * TOP_K))
    sorted_tok = jnp.where(jnp.arange(tm, dtype=I32)[None, :] < tile_rows[:, None], sorted_tok, 0)
    pos_steps = pos.reshape(T // tq, tq, TOP_K).transpose(0, 2, 1).reshape(T // tq, TOP_K * tq)
    return sorted_tok.astype(I32), tile_expert, tile_sub, pos_steps.astype(I32)


def _sub_block_loops(nsub, n_sub_max, sub, compute, zero):
    def pair_body(s, carry):
        compute(pl.multiple_of(s * 2 * sub, 2 * sub), 2 * sub)
        return carry

    lax.fori_loop(0, nsub // 2, pair_body, 0)

    @pl.when(nsub % 2 == 1)
    def _():
        compute(pl.multiple_of((nsub - 1) * sub, sub), sub)

    def zero_body(s, carry):
        zero(pl.multiple_of(s * sub, sub), sub)
        return carry

    lax.fori_loop(nsub, n_sub_max, zero_body, 0)


def _moe_up_kernel(te_ref, ts_ref, tok_hbm, u2_hbm, wg_ref, wl_ref, bg_ref, bl_ref, o_ref,
                   idx_smem, gbuf, xs_ref, wcat_ref, idx_sem, row_sem):
    t = pl.program_id(0)
    j = pl.program_id(1)
    n_tiles = pl.num_programs(0)
    tm = xs_ref.shape[0]
    half = gbuf.shape[1]
    tf = o_ref.shape[1]
    nsub = ts_ref[t]

    def idx_copy(tile, slot):
        return pltpu.make_async_copy(tok_hbm.at[tile], idx_smem.at[pl.ds(slot * tm, tm)], idx_sem.at[slot])

    def issue_rows(tile, slot):
        def body(g, carry):
            base = pl.multiple_of(g * DMA_UNROLL, DMA_UNROLL)
            for u in range(DMA_UNROLL):
                tok = idx_smem[slot * tm + base + u]
                pltpu.make_async_copy(u2_hbm.at[pl.ds(tok, 1)], gbuf.at[pl.ds(base + u, 1)], row_sem).start()
            return carry
        lax.fori_loop(0, ts_ref[tile] * (MOE_SUB // DMA_UNROLL), body, 0)

    def wait_sub_block(s, carry):
        pltpu.make_async_copy(u2_hbm.at[pl.ds(0, MOE_SUB)], gbuf.at[pl.ds(0, MOE_SUB)], row_sem).wait()
        return carry

    def unpack_sub_block(s, carry):
        r0 = pl.multiple_of(s * MOE_SUB, MOE_SUB)
        lo, hi = _unpack_bf16_pair(gbuf[pl.ds(r0, MOE_SUB), :])
        xs_ref[pl.ds(r0, MOE_SUB), :half] = lo
        xs_ref[pl.ds(r0, MOE_SUB), half:] = hi
        return carry

    @pl.when(j == 0)
    def _():
        slot = t % 2

        @pl.when(t == 0)
        def _():
            idx_copy(0, 0).start()
            idx_copy(0, 0).wait()
            issue_rows(0, 0)

            @pl.when(n_tiles > 1)
            def _():
                idx_copy(1, 1).start()

        lax.fori_loop(0, nsub, wait_sub_block, 0)
        lax.fori_loop(0, nsub, unpack_sub_block, 0)

        @pl.when(t + 1 < n_tiles)
        def _():
            idx_copy(t + 1, 1 - slot).wait()
            issue_rows(t + 1, 1 - slot)

        @pl.when(t + 2 < n_tiles)
        def _():
            idx_copy(t + 2, slot).start()

    @pl.when(nsub > 0)
    def _():
        wcat_ref[:, :tf] = wg_ref[0].astype(BF16)
        wcat_ref[:, tf:] = wl_ref[0].astype(BF16)

    def compute(r0, rows):
        h = jnp.dot(xs_ref[pl.ds(r0, rows), :], wcat_ref[...], preferred_element_type=F32)
        glu = jnp.minimum(h[:, :tf] + bg_ref[0], SWIGLU_LIMIT)
        lin = jnp.clip(h[:, tf:] + bl_ref[0], -SWIGLU_LIMIT, SWIGLU_LIMIT)
        act = glu * (1.0 / (1.0 + jnp.exp(-SWIGLU_ALPHA * glu))) * (lin + 1.0)
        o_ref[pl.ds(r0, rows), :] = act.astype(BF16)

    def zero(r0, rows):
        o_ref[pl.ds(r0, rows), :] = jnp.zeros((rows, tf), BF16)

    _sub_block_loops(nsub, tm // MOE_SUB, MOE_SUB, compute, zero)


def _moe_up(tile_expert, tile_sub, sorted_tok, u2p, w1, b1):
    E, D, F2 = w1.shape
    F = F2 // 2
    n_tiles, tm = sorted_tok.shape
    tf = min(MOE_TF, F)
    nj = F // tf

    def jj(t, j, te, ts):
        return jnp.where(ts[t] > 0, j, nj - 1)

    grid_spec = pltpu.PrefetchScalarGridSpec(
        num_scalar_prefetch=2,
        grid=(n_tiles, nj),
        in_specs=[pl.BlockSpec(memory_space=pl.ANY),
                  pl.BlockSpec(memory_space=pl.ANY),
                  pl.BlockSpec((1, D, tf), lambda t, j, te, tv: (te[t], 0, jj(t, j, te, tv))),
                  pl.BlockSpec((1, D, tf), lambda t, j, te, tv: (te[t], 0, nj + jj(t, j, te, tv))),
                  pl.BlockSpec((1, 1, tf), lambda t, j, te, tv: (te[t], 0, jj(t, j, te, tv))),
                  pl.BlockSpec((1, 1, tf), lambda t, j, te, tv: (te[t], 0, nj + jj(t, j, te, tv)))],
        out_specs=pl.BlockSpec((tm, tf), lambda t, j, te, tv: (t, j)),
        scratch_shapes=[pltpu.SMEM((2 * tm,), I32),
                        pltpu.VMEM((tm, D // 2), I32),
                        pltpu.VMEM((tm, D), BF16),
                        pltpu.VMEM((D, 2 * tf), BF16),
                        pltpu.SemaphoreType.DMA((2,)),
                        pltpu.SemaphoreType.DMA(())],
    )
    return pl.pallas_call(
        _moe_up_kernel,
        out_shape=jax.ShapeDtypeStruct((n_tiles * tm, F), BF16),
        grid_spec=grid_spec,
        compiler_params=_cparams(("arbitrary", "arbitrary")),
        name="moe_up",
    )(tile_expert, tile_sub, sorted_tok, u2p, w1, w1, b1.reshape(E, 1, F2), b1.reshape(E, 1, F2))


def _moe_down_kernel(te_ref, ts_ref, a_ref, w_ref, b_ref, o_ref, wb_ref):
    nsub = ts_ref[pl.program_id(0)]
    hw = o_ref.shape[1]

    @pl.when(nsub > 0)
    def _():
        wb_ref[...] = w_ref[0].astype(BF16)

    def compute(r0, rows):
        y = jnp.dot(a_ref[pl.ds(r0, rows), :], wb_ref[...], preferred_element_type=F32) + b_ref[0]
        o_ref[pl.ds(r0, rows), :] = _pack_bf16_pair(y[:, :hw], y[:, hw:])

    def zero(r0, rows):
        o_ref[pl.ds(r0, rows), :] = jnp.zeros((rows, hw), I32)

    _sub_block_loops(nsub, a_ref.shape[0] // MOE_SUB, MOE_SUB, compute, zero)


def _moe_down(tile_expert, tile_sub, act, w2, b2, tm):
    E, F, D = w2.shape
    rows = act.shape[0]
    n_tiles = rows // tm
    tn = min(MOE_TN, D)
    nj = D // tn

    def jj(t, j, te, ts):
        return jnp.where(ts[t] > 0, j, nj - 1)

    grid_spec = pltpu.PrefetchScalarGridSpec(
        num_scalar_prefetch=2,
        grid=(n_tiles, nj),
        in_specs=[pl.BlockSpec((tm, F), lambda t, j, te, ts: (t, 0)),
                  pl.BlockSpec((1, F, tn), lambda t, j, te, ts: (te[t], 0, jj(t, j, te, ts))),
                  pl.BlockSpec((1, 1, tn), lambda t, j, te, ts: (te[t], 0, jj(t, j, te, ts)))],
        out_specs=pl.BlockSpec((tm, tn // 2), lambda t, j, te, ts: (t, j)),
        scratch_shapes=[pltpu.VMEM((F, tn), BF16)],
    )
    return pl.pallas_call(
        _moe_down_kernel,
        out_shape=jax.ShapeDtypeStruct((rows, D // 2), I32),
        grid_spec=grid_spec,
        compiler_params=_cparams(("arbitrary", "arbitrary")),
        name="moe_down",
    )(tile_expert, tile_sub, act, w2, b2.reshape(E, 1, D))


def _combine_kernel(pos_hbm, y_hbm, gate_ref, x1_ref, mod_ref, g_ref, b_ref, o_ref,
                    idx_smem, ybuf, idx_sem, row_sem, *, tn):
    s = pl.program_id(0)
    n_steps = pl.num_programs(0)
    tq = x1_ref.shape[0]
    n_rows = TOP_K * tq
    slot = s % 2
    nxt = 1 - slot

    def idx_copy(step, sl):
        return pltpu.make_async_copy(pos_hbm.at[step], idx_smem.at[pl.ds(sl * n_rows, n_rows)], idx_sem.at[sl])

    def issue_rows(sl):
        def body(g, carry):
            base = pl.multiple_of(g * DMA_UNROLL, DMA_UNROLL)
            for u in range(DMA_UNROLL):
                row = idx_smem[sl * n_rows + base + u]
                pltpu.make_async_copy(y_hbm.at[pl.ds(row, 1)], ybuf.at[pl.ds(sl * n_rows + base + u, 1)],
                                      row_sem.at[sl]).start()
            return carry
        lax.fori_loop(0, n_rows // DMA_UNROLL, body, 0)

    @pl.when(s == 0)
    def _():
        idx_copy(0, 0).start()
        idx_copy(0, 0).wait()
        issue_rows(0)

        @pl.when(n_steps > 1)
        def _():
            idx_copy(1, 1).start()

    @pl.when(s + 1 < n_steps)
    def _():
        idx_copy(s + 1, nxt).wait()
        issue_rows(nxt)

    pltpu.make_async_copy(y_hbm.at[pl.ds(0, n_rows)], ybuf.at[pl.ds(slot * n_rows, n_rows)],
                          row_sem.at[slot]).wait()

    @pl.when(s + 2 < n_steps)
    def _():
        idx_copy(s + 2, slot).start()

    gate2 = mod_ref[0, 5:6, :]
    hw = tn // 2
    for r0 in range(0, tq, ROW_CHUNK):
        f_lo = f_hi = None
        for k in range(TOP_K):
            wu = pltpu.bitcast(ybuf[pl.ds(slot * n_rows + k * tq + r0, ROW_CHUNK), :], jnp.uint32)
            gk = gate_ref[r0:r0 + ROW_CHUNK, k:k + 1]
            lo = gk * pltpu.bitcast(wu << 16, F32)
            hi = gk * pltpu.bitcast(wu & jnp.uint32(0xFFFF0000), F32)
            f_lo = lo if f_lo is None else f_lo + lo
            f_hi = hi if f_hi is None else f_hi + hi
        parts = []
        for c0 in range(0, f_lo.shape[1], hw):
            parts += [f_lo[:, c0:c0 + hw], f_hi[:, c0:c0 + hw]]
        f = jnp.concatenate(parts, axis=-1)
        h = DEEPNORM_ALPHA * x1_ref[r0:r0 + ROW_CHUNK, :] + (1.0 + gate2) * f
        mu = jnp.mean(h, axis=-1, keepdims=True)
        hc = h - mu
        var = jnp.mean(jnp.square(hc), axis=-1, keepdims=True)
        o_ref[r0:r0 + ROW_CHUNK, :] = hc * lax.rsqrt(var + LN_EPS) * g_ref[...] + b_ref[...]


def _combine(pos_steps, y, gates, x1, mod, ln_g, ln_b, seq, tn):
    T, D = x1.shape
    n_steps = pos_steps.shape[0]
    tq = T // n_steps
    steps_per_batch = seq // tq
    row = lambda s: (0, 0)
    return pl.pallas_call(
        functools.partial(_combine_kernel, tn=tn),
        out_shape=jax.ShapeDtypeStruct((T, D), F32),
        grid=(n_steps,),
        in_specs=[pl.BlockSpec(memory_space=pl.ANY),
                  pl.BlockSpec(memory_space=pl.ANY),
                  pl.BlockSpec((tq, TOP_K), lambda s: (s, 0)),
                  pl.BlockSpec((tq, D), lambda s: (s, 0)),
                  pl.BlockSpec((1, 6, D), lambda s: (s // steps_per_batch, 0, 0)),
                  pl.BlockSpec((1, D), row), pl.BlockSpec((1, D), row)],
        out_specs=pl.BlockSpec((tq, D), lambda s: (s, 0)),
        scratch_shapes=[pltpu.SMEM((2 * TOP_K * tq,), I32),
                        pltpu.VMEM((2 * TOP_K * tq, D // 2), I32),
                        pltpu.SemaphoreType.DMA((2,)),
                        pltpu.SemaphoreType.DMA((2,))],
        compiler_params=_cparams(("arbitrary",)),
        name="combine",
    )(pos_steps, y, gates, x1, mod, ln_g.reshape(1, D), ln_b.reshape(1, D))


def kernel(x, c, rel_bias, w_mod, b_mod, w_in, b_in, lambda_q1, lambda_k1, lambda_q2, lambda_k2, subln_g, sinks,
           w_out, b_out, ln1_g, ln1_b, w_router, b_router, w1, b1, w2, b2, ln2_g, ln2_b):
    B, S, D = x.shape
    T = B * S
    E = w_router.shape[2]
    n_heads_diff = (D // 2) // (2 * HEAD_DIM_DIFF)
    n_heads_swa = (D // 2) // HEAD_DIM_SWA
    t_att = min(ATT_TILE, S)
    for l in range(DEPTH):
        lambda_init = 0.8 - 0.6 * math.exp(-0.3 * l)
        mod = _mod(c, w_mod[l], b_mod[l]).reshape(B, 6, D)
        proj = _in_proj(x, mod, w_in[l].astype(BF16), b_in[l])
        bias_d = _bias_tiles(rel_bias, 0, n_heads_diff, 2, t_att, t_att, True)
        bias_s = _bias_tiles(rel_bias, n_heads_diff, n_heads_swa, 2, WINDOW, 2 * WINDOW, False)
        lam_vecs = jnp.stack([lambda_q1[l], lambda_k1[l], lambda_q2[l], lambda_k2[l]]).astype(F32)
        a_diff = _diff_attn(proj, bias_d, lam_vecs, subln_g[l], n_heads_diff, lambda_init)
        a_swa = _swa_attn(proj, bias_s, sinks[l], n_heads_diff, n_heads_swa)
        x1, u2p, logits = _out_proj(a_diff, a_swa, w_out[l].astype(BF16), x, mod, b_out[l], ln1_g[l], ln1_b[l],
                                    w_router[l].astype(BF16), b_router[l])
        top_idx, gates = _router(logits.reshape(T, E))
        tm = MOE_TM
        assert (T * TOP_K) % tm == 0 and tm % (2 * MOE_SUB) == 0
        n_tiles = (T * TOP_K) // tm + E
        tq = min(CMB_TQ, S)
        sorted_tok, tile_expert, tile_sub, pos_steps = _routing_tables(top_idx, E, tm, MOE_SUB, n_tiles, tq)
        act = _moe_up(tile_expert, tile_sub, sorted_tok, u2p.reshape(T, D // 2), w1[l], b1[l])
        y = _moe_down(tile_expert, tile_sub, act, w2[l], b2[l], tm)
        x = _combine(pos_steps, y, gates, x1.reshape(T, D), mod, ln2_g[l], ln2_b[l], S,
                     min(MOE_TN, D)).reshape(B, S, D)
    return x
```

```python
import functools
import math

import jax
import jax.numpy as jnp
from jax import lax
from jax.experimental import pallas as pl
from jax.experimental.pallas import tpu as pltpu

F32 = jnp.float32
BF16 = jnp.bfloat16
I32 = jnp.int32

HEAD_DIM_DIFF = 128
HEAD_DIM_SWA = 64
SWA_GROUP = 8
WINDOW = 128
N_BUCKETS = 32
MAX_DISTANCE = 128
TOP_K = 4
SWIGLU_ALPHA = 1.702
SWIGLU_LIMIT = 7.0
LN_EPS = 1e-5
RMS_EPS = 1e-5
DEPTH = 1
DEEPNORM_ALPHA = (2.0 * DEPTH) ** 0.25
NEG_BIG = -1e30
LOG2E = 1.4426950408889634

LANES = 128
VMEM_LIMIT = 56 * 1024 * 1024

ATT_TILE = 512
ATT_SLAB = 32
PROJ_TM = 1024
PROJ_TN = 512
OUT_TM = 512
OUT_TK = 512
ROW_CHUNK = 128
MOE_TM = 2304
MOE_SUB = 256
MOE_TF = 128
MOE_TN = 1024
DMA_UNROLL = 8
CMB_TQ = 256
ROUTER_TR = 2048


def _cparams(sem):
    return pltpu.CompilerParams(dimension_semantics=sem, vmem_limit_bytes=VMEM_LIMIT)


def _mod_kernel(c_ref, w_ref, b_ref, o_ref):
    c = c_ref[...]
    s = c * (1.0 / (1.0 + jnp.exp(-c)))
    acc = jnp.dot(s.astype(BF16), w_ref[...].astype(BF16), preferred_element_type=F32)
    o_ref[...] = acc + b_ref[...]


def _mod(c, w_mod, b_mod, tn=512):
    B, D = c.shape
    N = w_mod.shape[1]
    return pl.pallas_call(
        _mod_kernel,
        out_shape=jax.ShapeDtypeStruct((B, N), F32),
        grid=(N // tn,),
        in_specs=[pl.BlockSpec((B, D), lambda j: (0, 0)),
                  pl.BlockSpec((D, tn), lambda j: (0, j)),
                  pl.BlockSpec((1, tn), lambda j: (0, j))],
        out_specs=pl.BlockSpec((B, tn), lambda j: (0, j)),
        compiler_params=_cparams(("arbitrary",)),
        name="mod",
    )(c, w_mod, b_mod.reshape(1, N))


def _in_proj_kernel(x_hbm, mod_ref, w_ref, b_ref, o_ref, xbuf, u_ref, sem):
    i, j = pl.program_id(1), pl.program_id(2)
    ni = pl.num_programs(1)
    tm = xbuf.shape[0]
    step = pl.program_id(0) * ni + i
    n_steps = pl.num_programs(0) * ni

    def x_copy(s):
        return pltpu.make_async_copy(x_hbm.at[s // ni, pl.ds((s % ni) * tm, tm)], xbuf, sem)

    @pl.when(j == 0)
    def _():
        @pl.when(step == 0)
        def _():
            x_copy(0).start()

        x_copy(step).wait()
        shift = mod_ref[0, 0:1, :]
        scale = mod_ref[0, 1:2, :]
        for r0 in range(0, tm, ROW_CHUNK):
            u_ref[r0:r0 + ROW_CHUNK, :] = (xbuf[r0:r0 + ROW_CHUNK, :] * (1.0 + scale) + shift).astype(BF16)

    @pl.when((j == 1) & (step + 1 < n_steps))
    def _():
        x_copy(step + 1).start()

    acc = jnp.dot(u_ref[...], w_ref[...], preferred_element_type=F32)
    o_ref[0] = (acc + b_ref[...]).astype(BF16)


def _in_proj(x, mod, w_in_bf, b_in):
    B, S, D = x.shape
    N = w_in_bf.shape[1]
    tm, tn = min(PROJ_TM, S), PROJ_TN
    assert N // tn >= 2
    return pl.pallas_call(
        _in_proj_kernel,
        out_shape=jax.ShapeDtypeStruct((B, S, N), BF16),
        grid=(B, S // tm, N // tn),
        in_specs=[pl.BlockSpec(memory_space=pl.ANY),
                  pl.BlockSpec((1, 6, D), lambda b, i, j: (b, 0, 0)),
                  pl.BlockSpec((D, tn), lambda b, i, j: (0, j)),
                  pl.BlockSpec((1, tn), lambda b, i, j: (0, j))],
        out_specs=pl.BlockSpec((1, tm, tn), lambda b, i, j: (b, i, j)),
        scratch_shapes=[pltpu.VMEM((tm, D), F32), pltpu.VMEM((tm, D), BF16), pltpu.SemaphoreType.DMA(())],
        compiler_params=_cparams(("arbitrary", "arbitrary", "arbitrary")),
        name="in_proj",
    )(x, mod, w_in_bf, b_in.reshape(1, N))


def _bias_kernel(table_ref, o_ref, *, col0, causal):
    h = pl.program_id(0) + col0
    v = pl.program_id(1)
    rows, cols = o_ref.shape[2], o_ref.shape[3]
    qi = lax.broadcasted_iota(I32, (rows, cols), 0)
    kj = lax.broadcasted_iota(I32, (rows, cols), 1)
    if causal:
        dist = qi - kj + v * rows
        valid = dist >= 0
    else:
        dist = qi - kj + rows
        valid = (dist >= 0) & (dist < WINDOW) & jnp.logical_not((v == 1) & (kj < rows))
    n = jnp.maximum(dist, 0)
    max_exact = N_BUCKETS // 2
    nf = jnp.maximum(n, 1).astype(F32)
    large = max_exact + (jnp.log(nf / max_exact) / math.log(MAX_DISTANCE / max_exact)
                         * (N_BUCKETS - max_exact)).astype(I32)
    large = jnp.minimum(large, N_BUCKETS - 1)
    bucket = jnp.where(n < max_exact, n, large)
    val = jnp.zeros((rows, cols), F32)
    for bkt in range(N_BUCKETS):
        val = jnp.where(bucket == bkt, table_ref[bkt, h], val)
    if causal:
        val = val - table_ref[N_BUCKETS - 1, h]
    o_ref[0, 0] = jnp.where(valid, val * LOG2E, NEG_BIG)


def _bias_tiles(rel_bias, col0, n_heads, n_var, rows, cols, causal):
    return pl.pallas_call(
        functools.partial(_bias_kernel, col0=col0, causal=causal),
        out_shape=jax.ShapeDtypeStruct((n_heads, n_var, rows, cols), F32),
        grid=(n_heads, n_var),
        in_specs=[pl.BlockSpec(memory_space=pltpu.SMEM)],
        out_specs=pl.BlockSpec((1, 1, rows, cols), lambda h, v: (h, v, 0, 0)),
        compiler_params=_cparams(("arbitrary", "arbitrary")),
        name="bias_tiles",
    )(rel_bias)


def _diff_attn_kernel(q_ref, k_ref, v_ref, bias_ref, lam_ref, g_ref, o_ref,
                      m_ref, l_ref, acc_ref, s_ref, p_ref, a_ref, *, lambda_init):
    t = q_ref.shape[1]
    d = HEAD_DIM_DIFF
    qi = pl.program_id(2)
    q = (q_ref[0].astype(F32) * (d ** -0.5 * LOG2E)).astype(BF16)

    lv = lam_ref[...]
    lam = (jnp.exp(jnp.sum(lv[0:1] * lv[1:2], axis=-1, keepdims=True))
           - jnp.exp(jnp.sum(lv[2:3] * lv[3:4], axis=-1, keepdims=True)) + lambda_init)

    nt = (((1,), (1,)), ((), ()))
    m_ref[...] = jnp.full(m_ref.shape, NEG_BIG, F32)
    l_ref[...] = jnp.zeros(l_ref.shape, F32)
    acc_ref[...] = jnp.zeros(acc_ref.shape, F32)

    def block(kj, var):
        k = k_ref[0, pl.ds(pl.multiple_of(kj * t, t), t), :]
        v = v_ref[0, pl.ds(pl.multiple_of(kj * t, t), t), :]
        for c in range(2):
            s_ref[c] = lax.dot_general(q[:, c * d:(c + 1) * d], k[:, c * d:(c + 1) * d], nt,
                                       preferred_element_type=F32)
        for c in range(2):
            for r0 in range(0, t, ATT_SLAB):
                rs = slice(r0, r0 + ATT_SLAB)
                ch = []
                for j0 in range(0, t, LANES):
                    sj = s_ref[c, rs, j0:j0 + LANES]
                    if var is not None:
                        sj = sj + bias_ref[0, var, rs, j0:j0 + LANES]
                    ch.append(sj)
                m_old = m_ref[c, rs, :]
                m_new = jnp.maximum(m_old, jnp.max(functools.reduce(jnp.maximum, ch), axis=-1, keepdims=True))
                alpha = jnp.exp2(m_old - m_new)
                ps = [jnp.exp2(sj - m_new) for sj in ch]
                l_ref[c, rs, :] = alpha * l_ref[c, rs, :] + functools.reduce(jnp.add, ps)
                m_ref[c, rs, :] = m_new
                a_ref[c, rs, :] = alpha
                for j, pj in enumerate(ps):
                    p_ref[c, rs, j * LANES:(j + 1) * LANES] = pj.astype(BF16)
            al = a_ref[c]
            acc_ref[c] = (acc_ref[c] * jnp.concatenate([al] * (2 * d // LANES), axis=-1)
                          + jnp.dot(p_ref[c], v, preferred_element_type=F32))

    def far_step(kj, carry):
        block(kj, None)
        return carry

    def near_step(kj, carry):
        block(kj, qi - kj)
        return carry

    n_far = jnp.maximum(qi - 1, 0)
    lax.fori_loop(0, n_far, far_step, 0)
    lax.fori_loop(n_far, qi + 1, near_step, 0)

    l0 = jnp.sum(l_ref[0], axis=-1, keepdims=True)
    l1 = jnp.sum(l_ref[1], axis=-1, keepdims=True)
    o = acc_ref[0] / l0 - lam * (acc_ref[1] / l1)
    o = o * lax.rsqrt(jnp.mean(jnp.square(o), axis=-1, keepdims=True) + RMS_EPS) * g_ref[...]
    o_ref[0] = (o * (1.0 - lambda_init)).astype(BF16)


def _diff_attn(proj, bias, lam_vecs, subln_g, n_heads, lambda_init):
    B, S, _ = proj.shape
    t = min(ATT_TILE, S)
    w = 2 * HEAD_DIM_DIFF
    return pl.pallas_call(
        functools.partial(_diff_attn_kernel, lambda_init=lambda_init),
        out_shape=jax.ShapeDtypeStruct((B, S, n_heads * w), BF16),
        grid=(B, n_heads, S // t),
        in_specs=[pl.BlockSpec((1, t, w), lambda b, h, i: (b, i, h)),
                  pl.BlockSpec((1, S, w), lambda b, h, i: (b, 0, n_heads + h)),
                  pl.BlockSpec((1, S, w), lambda b, h, i: (b, 0, 2 * n_heads + h)),
                  pl.BlockSpec((1, 2, t, t), lambda b, h, i: (h, 0, 0, 0)),
                  pl.BlockSpec((4, HEAD_DIM_DIFF), lambda b, h, i: (0, 0)),
                  pl.BlockSpec((1, w), lambda b, h, i: (0, 0))],
        out_specs=pl.BlockSpec((1, t, w), lambda b, h, i: (b, i, h)),
        scratch_shapes=[pltpu.VMEM((2, t, LANES), F32), pltpu.VMEM((2, t, LANES), F32), pltpu.VMEM((2, t, w), F32),
                        pltpu.VMEM((2, t, t), F32), pltpu.VMEM((2, t, t), BF16), pltpu.VMEM((2, t, LANES), F32)],
        compiler_params=_cparams(("arbitrary", "arbitrary", "arbitrary")),
        name="diff_attn",
    )(proj, proj, proj, bias, lam_vecs, subln_g.reshape(1, w))


def _swa_kernel(sink_ref, q_ref, kc_ref, vc_ref, kp_ref, vp_ref, bias_ref, o_ref, *, n_kv):
    d = HEAD_DIM_SWA
    var = jnp.where(pl.program_id(1) == 0, 1, 0)
    q = (q_ref[0].astype(F32) * (d ** -0.5 * LOG2E)).astype(BF16)
    kk = jnp.concatenate([kp_ref[0], kc_ref[0]], axis=0)
    vv = jnp.concatenate([vp_ref[0], vc_ref[0]], axis=0)
    lo_half = lax.broadcasted_iota(I32, (kk.shape[0], LANES), 1) < d
    zero = jnp.zeros((kk.shape[0], LANES), BF16)
    nt = (((1,), (1,)), ((), ()))

    def halves(x2, e):
        swapped = jnp.concatenate([x2[:, d:], x2[:, :d]], axis=1)
        own_lo = x2 if e == 0 else swapped
        own_hi = swapped if e == 0 else x2
        return jnp.where(lo_half, own_lo, zero), jnp.where(lo_half, zero, own_hi)

    for h in range(n_kv):
        pr, e = divmod(h, 2)
        k_lo, k_hi = halves(kk[:, pr * LANES:(pr + 1) * LANES], e)
        v_lo, v_hi = halves(vv[:, pr * LANES:(pr + 1) * LANES], e)
        heads = [h * SWA_GROUP + g for g in range(SWA_GROUP)]
        sinks = [sink_ref[hq] * LOG2E for hq in heads]
        s = [lax.dot_general(q[:, (hq // 2) * LANES:(hq // 2 + 1) * LANES], k_lo if hq % 2 == 0 else k_hi, nt,
                             preferred_element_type=F32) + bias_ref[hq, var] for hq in heads]
        m = [jnp.maximum(jnp.max(s_g, axis=-1, keepdims=True), sk) for s_g, sk in zip(s, sinks)]
        p = [jnp.exp2(s_g - m_g) for s_g, m_g in zip(s, m)]
        denom = [jnp.sum(p_g, axis=-1, keepdims=True) + jnp.exp2(sk - m_g) for p_g, m_g, sk in zip(p, m, sinks)]
        pn = [(p_g * (1.0 / d_g)).astype(BF16) for p_g, d_g in zip(p, denom)]
        for gp in range(SWA_GROUP // 2):
            col = (heads[2 * gp] // 2) * LANES
            o = (jnp.dot(pn[2 * gp], v_lo, preferred_element_type=F32)
                 + jnp.dot(pn[2 * gp + 1], v_hi, preferred_element_type=F32))
            o_ref[0, :, col:col + LANES] = o.astype(BF16)


def _swa_attn(proj, bias, sinks, n_heads_diff, n_heads_swa):
    B, S, _ = proj.shape
    n_kv = n_heads_swa // SWA_GROUP
    r = WINDOW
    qw = n_heads_swa * HEAD_DIM_SWA
    kw = n_kv * HEAD_DIM_SWA
    q_blk = (3 * n_heads_diff * 2 * HEAD_DIM_DIFF) // qw
    k_blk = (3 * n_heads_diff * 2 * HEAD_DIM_DIFF + qw) // kw
    return pl.pallas_call(
        functools.partial(_swa_kernel, n_kv=n_kv),
        out_shape=jax.ShapeDtypeStruct((B, S, qw), BF16),
        grid=(B, S // r),
        in_specs=[pl.BlockSpec(memory_space=pltpu.SMEM),
                  pl.BlockSpec((1, r, qw), lambda b, i: (b, i, q_blk)),
                  pl.BlockSpec((1, r, kw), lambda b, i: (b, i, k_blk)),
                  pl.BlockSpec((1, r, kw), lambda b, i: (b, i, k_blk + 1)),
                  pl.BlockSpec((1, r, kw), lambda b, i: (b, jnp.maximum(i - 1, 0), k_blk)),
                  pl.BlockSpec((1, r, kw), lambda b, i: (b, jnp.maximum(i - 1, 0), k_blk + 1)),
                  pl.BlockSpec((n_heads_swa, 2, r, 2 * r), lambda b, i: (0, 0, 0, 0))],
        out_specs=pl.BlockSpec((1, r, qw), lambda b, i: (b, i, 0)),
        compiler_params=_cparams(("arbitrary", "arbitrary")),
        name="swa_attn",
    )(sinks, proj, proj, proj, proj, proj, bias)


def _pack_bf16_pair(lo, hi):
    lo_b = pltpu.bitcast(lo.astype(BF16).astype(F32), jnp.uint32) >> 16
    hi_b = pltpu.bitcast(hi.astype(BF16).astype(F32), jnp.uint32) & jnp.uint32(0xFFFF0000)
    return pltpu.bitcast(lo_b | hi_b, I32)


def _unpack_bf16_pair(w):
    wu = pltpu.bitcast(w, jnp.uint32)
    lo = pltpu.bitcast(wu << 16, F32).astype(BF16)
    hi = pltpu.bitcast(wu & jnp.uint32(0xFFFF0000), F32).astype(BF16)
    return lo, hi


def _out_proj_kernel(ad_ref, as_ref, w_ref, x_hbm, mod_ref, bo_ref, g_ref, b_ref, wr_ref, br_ref,
                     x1_hbm, u2_ref, lg_ref, acc_ref, xbuf, x1buf, x_sem, o_sem, *, n_kd):
    i, k = pl.program_id(1), pl.program_id(2)
    ni, nk = pl.num_programs(1), pl.num_programs(2)
    tm = xbuf.shape[0]
    step = pl.program_id(0) * ni + i
    n_steps = pl.num_programs(0) * ni

    def x_copy(s):
        return pltpu.make_async_copy(x_hbm.at[s // ni, pl.ds((s % ni) * tm, tm)], xbuf, x_sem)

    def o_copy(s):
        return pltpu.make_async_copy(x1buf, x1_hbm.at[s // ni, pl.ds((s % ni) * tm, tm)], o_sem)

    @pl.when(k == 0)
    def _():
        x_copy(step).start()
        acc_ref[...] = jnp.dot(ad_ref[0], w_ref[...], preferred_element_type=F32)

    @pl.when((k > 0) & (k < n_kd))
    def _():
        acc_ref[...] += jnp.dot(ad_ref[0], w_ref[...], preferred_element_type=F32)

    @pl.when(k >= n_kd)
    def _():
        acc_ref[...] += jnp.dot(as_ref[0], w_ref[...], preferred_element_type=F32)

    @pl.when(k == nk - 1)
    def _():
        x_copy(step).wait()

        @pl.when(step > 0)
        def _():
            o_copy(step - 1).wait()

        gate1 = mod_ref[0, 2:3, :]
        shift2 = mod_ref[0, 3:4, :]
        scale2 = mod_ref[0, 4:5, :]
        half = xbuf.shape[1] // 2
        for r0 in range(0, tm, ROW_CHUNK):
            rs = slice(r0, r0 + ROW_CHUNK)
            a = acc_ref[rs, :] + bo_ref[...]
            h = DEEPNORM_ALPHA * xbuf[rs, :] + (1.0 + gate1) * a
            mu = jnp.mean(h, axis=-1, keepdims=True)
            hc = h - mu
            var = jnp.mean(jnp.square(hc), axis=-1, keepdims=True)
            x1 = hc * lax.rsqrt(var + LN_EPS) * g_ref[...] + b_ref[...]
            x1buf[rs, :] = x1
            u2 = x1 * (1.0 + scale2) + shift2
            lg_ref[0, rs, :] = jnp.dot(u2.astype(BF16), wr_ref[...], preferred_element_type=F32) + br_ref[...]
            u2_ref[0, rs, :] = _pack_bf16_pair(u2[:, :half], u2[:, half:])
        o_copy(step).start()

        @pl.when(step == n_steps - 1)
        def _():
            o_copy(step).wait()


def _out_proj(a_diff, a_swa, w_out_bf, x, mod, b_out, ln_g, ln_b, w_router_bf, b_router):
    B, S, D = x.shape
    E = w_router_bf.shape[1]
    tm, tk = min(OUT_TM, S), OUT_TK
    n_kd = a_diff.shape[2] // tk
    n_ks = a_swa.shape[2] // tk
    row = lambda b, i, k: (0, 0)
    return pl.pallas_call(
        functools.partial(_out_proj_kernel, n_kd=n_kd),
        out_shape=(jax.ShapeDtypeStruct((B, S, D), F32),
                   jax.ShapeDtypeStruct((B, S, D // 2), I32),
                   jax.ShapeDtypeStruct((B, S, E), F32)),
        grid=(B, S // tm, n_kd + n_ks),
        in_specs=[pl.BlockSpec((1, tm, tk), lambda b, i, k: (b, i, jnp.minimum(k, n_kd - 1))),
                  pl.BlockSpec((1, tm, tk), lambda b, i, k: (b, i, jnp.maximum(k - n_kd, 0))),
                  pl.BlockSpec((tk, D), lambda b, i, k: (k, 0)),
                  pl.BlockSpec(memory_space=pl.ANY),
                  pl.BlockSpec((1, 6, D), lambda b, i, k: (b, 0, 0)),
                  pl.BlockSpec((1, D), row), pl.BlockSpec((1, D), row), pl.BlockSpec((1, D), row),
                  pl.BlockSpec((D, E), row), pl.BlockSpec((1, E), row)],
        out_specs=(pl.BlockSpec(memory_space=pl.ANY),
                   pl.BlockSpec((1, tm, D // 2), lambda b, i, k: (b, i, 0)),
                   pl.BlockSpec((1, tm, E), lambda b, i, k: (b, i, 0))),
        scratch_shapes=[pltpu.VMEM((tm, D), F32), pltpu.VMEM((tm, D), F32), pltpu.VMEM((tm, D), F32),
                        pltpu.SemaphoreType.DMA(()), pltpu.SemaphoreType.DMA(())],
        compiler_params=_cparams(("arbitrary", "arbitrary", "arbitrary")),
        name="out_proj",
    )(a_diff, a_swa, w_out_bf, x, mod, b_out.reshape(1, D), ln_g.reshape(1, D), ln_b.reshape(1, D),
      w_router_bf, b_router.reshape(1, E))


def _router_kernel(lg_ref, idx_ref, gate_ref):
    l = lg_ref[...]
    n, e = l.shape
    eidx = lax.broadcasted_iota(I32, (n, e), 1)
    kcol = lax.broadcasted_iota(I32, (n, TOP_K), 1)
    idx_out = jnp.zeros((n, TOP_K), I32)
    val_out = jnp.zeros((n, TOP_K), F32)
    for r in range(TOP_K):
        m = jnp.max(l, axis=-1, keepdims=True)
        i = jnp.min(jnp.where(l == m, eidx, e), axis=-1, keepdims=True)
        idx_out = jnp.where(kcol == r, i, idx_out)
        val_out = jnp.where(kcol == r, m, val_out)
        l = jnp.where(eidx == i, -jnp.inf, l)
    ex = jnp.exp(val_out - jnp.max(val_out, axis=-1, keepdims=True))
    idx_ref[...] = idx_out
    gate_ref[...] = ex / jnp.sum(ex, axis=-1, keepdims=True)


def _router(logits):
    T, E = logits.shape
    tr = min(ROUTER_TR, T)
    return pl.pallas_call(
        _router_kernel,
        out_shape=(jax.ShapeDtypeStruct((T, TOP_K), I32), jax.ShapeDtypeStruct((T, TOP_K), F32)),
        grid=(T // tr,),
        in_specs=[pl.BlockSpec((tr, E), lambda i: (i, 0))],
        out_specs=(pl.BlockSpec((tr, TOP_K), lambda i: (i, 0)), pl.BlockSpec((tr, TOP_K), lambda i: (i, 0))),
        compiler_params=_cparams(("arbitrary",)),
        name="router",
    )(logits)


def _routing_tables(top_idx, n_experts, tm, sub, n_tiles, tq):
    T = top_idx.shape[0]
    flat_e = top_idx.reshape(-1)
    onehot = (flat_e[:, None] == jnp.arange(n_experts, dtype=I32)[None, :]).astype(I32)
    csum = jnp.cumsum(onehot, axis=0)
    rank = jnp.sum(csum * onehot, axis=1) - 1
    counts = csum[-1]
    tiles_per = (counts + tm - 1) // tm
    tile_end = jnp.cumsum(tiles_per)
    tile_start = tile_end - tiles_per
    n_used = tile_end[-1]
    pos = tile_start[flat_e] * tm + rank
    tile_ids = jnp.arange(n_tiles, dtype=I32)
    tile_expert = jnp.sum((tile_ids[:, None] >= tile_end[None, :]).astype(I32), axis=1)
    last_expert = jnp.max(jnp.where(counts > 0, jnp.arange(n_experts, dtype=I32), 0))
    tile_expert = jnp.where(tile_ids < n_used, tile_expert, last_expert).astype(I32)
    first_in_expert = (tile_ids - tile_start[tile_expert]) * tm
    tile_rows = jnp.where(tile_ids < n_used, jnp.clip(counts[tile_expert] - first_in_expert, 0, tm), 0)
    tile_sub = ((tile_rows + sub - 1) // sub).astype(I32)
    first_pair = (jnp.cumsum(counts) - counts)[tile_expert] + first_in_expert
    first_pair = jnp.clip(first_pair, 0, T * TOP_K).astype(I32)
    order = jnp.argsort(flat_e, stable=True).astype(I32)
    sorted_tok = jnp.concatenate([order // TOP_K, jnp.zeros((_idx_len(tm),), I32)])
    pos_steps = pos.reshape(T // tq, tq, TOP_K).transpose(0, 2, 1).reshape(T // tq, TOP_K * tq)
    return sorted_tok, first_pair, tile_expert, tile_sub, n_used.reshape(1).astype(I32), pos_steps.astype(I32)


def _sub_block_loops(nsub, n_sub_max, sub, compute, zero):
    def pair_body(s, carry):
        compute(pl.multiple_of(s * 2 * sub, 2 * sub), 2 * sub)
        return carry

    lax.fori_loop(0, nsub // 2, pair_body, 0)

    @pl.when(nsub % 2 == 1)
    def _():
        compute(pl.multiple_of((nsub - 1) * sub, sub), sub)

    def zero_body(s, carry):
        zero(pl.multiple_of(s * sub, sub), sub)
        return carry

    lax.fori_loop(nsub, n_sub_max, zero_body, 0)


def _idx_len(tm):
    return -(-(tm + LANES) // 1024) * 1024


def _moe_up_kernel(te_ref, ts_ref, fp_ref, tok_hbm, u2_hbm, wg_ref, wl_ref, bg_ref, bl_ref, o_ref,
                   idx_smem, gbuf, xs_ref, wcat_ref, idx_sem, row_sem):
    t = pl.program_id(0)
    j = pl.program_id(1)
    n_tiles = pl.num_programs(0)
    tm = xs_ref.shape[0]
    half = gbuf.shape[1]
    tf = o_ref.shape[1]
    nsub = ts_ref[t]
    il = _idx_len(tm)

    def idx_copy(tile, slot):
        start = pl.multiple_of((fp_ref[tile] // LANES) * LANES, LANES)
        return pltpu.make_async_copy(tok_hbm.at[pl.ds(start, il)], idx_smem.at[pl.ds(slot * il, il)],
                                     idx_sem.at[slot])

    def issue_rows(tile, slot):
        first = slot * il + fp_ref[tile] % LANES

        def body(g, carry):
            base = pl.multiple_of(g * DMA_UNROLL, DMA_UNROLL)
            for u in range(DMA_UNROLL):
                tok = idx_smem[first + base + u]
                pltpu.make_async_copy(u2_hbm.at[pl.ds(tok, 1)], gbuf.at[pl.ds(base + u, 1)], row_sem).start()
            return carry
        lax.fori_loop(0, ts_ref[tile] * (MOE_SUB // DMA_UNROLL), body, 0)

    def wait_sub_block(s, carry):
        pltpu.make_async_copy(u2_hbm.at[pl.ds(0, MOE_SUB)], gbuf.at[pl.ds(0, MOE_SUB)], row_sem).wait()
        return carry

    def unpack_sub_block(s, carry):
        r0 = pl.multiple_of(s * MOE_SUB, MOE_SUB)
        lo, hi = _unpack_bf16_pair(gbuf[pl.ds(r0, MOE_SUB), :])
        xs_ref[pl.ds(r0, MOE_SUB), :half] = lo
        xs_ref[pl.ds(r0, MOE_SUB), half:] = hi
        return carry

    @pl.when(j == 0)
    def _():
        slot = t % 2

        @pl.when(t == 0)
        def _():
            idx_copy(0, 0).start()
            idx_copy(0, 0).wait()
            issue_rows(0, 0)

            @pl.when(n_tiles > 1)
            def _():
                idx_copy(1, 1).start()

        lax.fori_loop(0, nsub, wait_sub_block, 0)
        lax.fori_loop(0, nsub, unpack_sub_block, 0)

        @pl.when(t + 1 < n_tiles)
        def _():
            idx_copy(t + 1, 1 - slot).wait()
            issue_rows(t + 1, 1 - slot)

        @pl.when(t + 2 < n_tiles)
        def _():
            idx_copy(t + 2, slot).start()

    @pl.when(nsub > 0)
    def _():
        wcat_ref[:, :tf] = wg_ref[0].astype(BF16)
        wcat_ref[:, tf:] = wl_ref[0].astype(BF16)

    def compute(r0, rows):
        h = jnp.dot(xs_ref[pl.ds(r0, rows), :], wcat_ref[...], preferred_element_type=F32)
        glu = jnp.minimum(h[:, :tf] + bg_ref[0], SWIGLU_LIMIT)
        lin = jnp.clip(h[:, tf:] + bl_ref[0], -SWIGLU_LIMIT, SWIGLU_LIMIT)
        act = glu * (1.0 / (1.0 + jnp.exp(-SWIGLU_ALPHA * glu))) * (lin + 1.0)
        o_ref[pl.ds(r0, rows), :] = act.astype(BF16)

    def zero(r0, rows):
        o_ref[pl.ds(r0, rows), :] = jnp.zeros((rows, tf), BF16)

    _sub_block_loops(nsub, tm // MOE_SUB, MOE_SUB, compute, zero)


def _moe_up(tile_expert, tile_sub, first_pair, sorted_tok, u2p, w1, b1, tm):
    E, D, F2 = w1.shape
    F = F2 // 2
    n_tiles = tile_expert.shape[0]
    tf = min(MOE_TF, F)
    nj = F // tf

    def jj(t, j, ts):
        return jnp.where(ts[t] > 0, j, nj - 1)

    grid_spec = pltpu.PrefetchScalarGridSpec(
        num_scalar_prefetch=3,
        grid=(n_tiles, nj),
        in_specs=[pl.BlockSpec(memory_space=pl.ANY),
                  pl.BlockSpec(memory_space=pl.ANY),
                  pl.BlockSpec((1, D, tf), lambda t, j, te, ts, fp: (te[t], 0, jj(t, j, ts))),
                  pl.BlockSpec((1, D, tf), lambda t, j, te, ts, fp: (te[t], 0, nj + jj(t, j, ts))),
                  pl.BlockSpec((1, 1, tf), lambda t, j, te, ts, fp: (te[t], 0, jj(t, j, ts))),
                  pl.BlockSpec((1, 1, tf), lambda t, j, te, ts, fp: (te[t], 0, nj + jj(t, j, ts)))],
        out_specs=pl.BlockSpec((tm, tf), lambda t, j, te, ts, fp: (t, j)),
        scratch_shapes=[pltpu.SMEM((2 * _idx_len(tm),), I32),
                        pltpu.VMEM((tm, D // 2), I32),
                        pltpu.VMEM((tm, D), BF16),
                        pltpu.VMEM((D, 2 * tf), BF16),
                        pltpu.SemaphoreType.DMA((2,)),
                        pltpu.SemaphoreType.DMA(())],
    )
    return pl.pallas_call(
        _moe_up_kernel,
        out_shape=jax.ShapeDtypeStruct((n_tiles * tm, F), BF16),
        grid_spec=grid_spec,
        compiler_params=_cparams(("arbitrary", "arbitrary")),
        name="moe_up",
    )(tile_expert, tile_sub, first_pair, sorted_tok, u2p, w1, w1, b1.reshape(E, 1, F2), b1.reshape(E, 1, F2))


def _moe_down_kernel(te_ref, ts_ref, nu_ref, a_ref, w_ref, b_ref, o_ref, wb_ref):
    nsub = ts_ref[pl.program_id(0)]
    hw = o_ref.shape[1]

    @pl.when(nsub > 0)
    def _():
        wb_ref[...] = w_ref[0].astype(BF16)

    def compute(r0, rows):
        y = jnp.dot(a_ref[pl.ds(r0, rows), :], wb_ref[...], preferred_element_type=F32) + b_ref[0]
        o_ref[pl.ds(r0, rows), :] = _pack_bf16_pair(y[:, :hw], y[:, hw:])

    def zero(r0, rows):
        o_ref[pl.ds(r0, rows), :] = jnp.zeros((rows, hw), I32)

    _sub_block_loops(nsub, a_ref.shape[0] // MOE_SUB, MOE_SUB, compute, zero)


def _moe_down(tile_expert, tile_sub, n_used, act, w2, b2, tm):
    E, F, D = w2.shape
    rows = act.shape[0]
    n_tiles = rows // tm
    tn = min(MOE_TN, D)
    nj = D // tn

    def jj(t, j, ts):
        return jnp.where(ts[t] > 0, j, nj - 1)

    def tt(t, nu):
        return jnp.minimum(t, nu[0] - 1)

    grid_spec = pltpu.PrefetchScalarGridSpec(
        num_scalar_prefetch=3,
        grid=(n_tiles, nj),
        in_specs=[pl.BlockSpec((tm, F), lambda t, j, te, ts, nu: (tt(t, nu), 0)),
                  pl.BlockSpec((1, F, tn), lambda t, j, te, ts, nu: (te[t], 0, jj(t, j, ts))),
                  pl.BlockSpec((1, 1, tn), lambda t, j, te, ts, nu: (te[t], 0, jj(t, j, ts)))],
        out_specs=pl.BlockSpec((tm, tn // 2), lambda t, j, te, ts, nu: (t, j)),
        scratch_shapes=[pltpu.VMEM((F, tn), BF16)],
    )
    return pl.pallas_call(
        _moe_down_kernel,
        out_shape=jax.ShapeDtypeStruct((rows, D // 2), I32),
        grid_spec=grid_spec,
        compiler_params=_cparams(("arbitrary", "arbitrary")),
        name="moe_down",
    )(tile_expert, tile_sub, n_used, act, w2, b2.reshape(E, 1, D))


def _combine_kernel(pos_hbm, y_hbm, gate_ref, x1_ref, mod_ref, g_ref, b_ref, o_ref,
                    idx_smem, ybuf, idx_sem, row_sem, *, tn):
    s = pl.program_id(0)
    n_steps = pl.num_programs(0)
    tq = x1_ref.shape[0]
    n_rows = TOP_K * tq
    slot = s % 2
    nxt = 1 - slot

    def idx_copy(step, sl):
        return pltpu.make_async_copy(pos_hbm.at[step], idx_smem.at[pl.ds(sl * n_rows, n_rows)], idx_sem.at[sl])

    def issue_rows(sl):
        def body(g, carry):
            base = pl.multiple_of(g * DMA_UNROLL, DMA_UNROLL)
            for u in range(DMA_UNROLL):
                row = idx_smem[sl * n_rows + base + u]
                pltpu.make_async_copy(y_hbm.at[pl.ds(row, 1)], ybuf.at[pl.ds(sl * n_rows + base + u, 1)],
                                      row_sem.at[sl]).start()
            return carry
        lax.fori_loop(0, n_rows // DMA_UNROLL, body, 0)

    @pl.when(s == 0)
    def _():
        idx_copy(0, 0).start()
        idx_copy(0, 0).wait()
        issue_rows(0)

        @pl.when(n_steps > 1)
        def _():
            idx_copy(1, 1).start()

    @pl.when(s + 1 < n_steps)
    def _():
        idx_copy(s + 1, nxt).wait()
        issue_rows(nxt)

    pltpu.make_async_copy(y_hbm.at[pl.ds(0, n_rows)], ybuf.at[pl.ds(slot * n_rows, n_rows)],
                          row_sem.at[slot]).wait()

    @pl.when(s + 2 < n_steps)
    def _():
        idx_copy(s + 2, slot).start()

    gate2 = mod_ref[0, 5:6, :]
    hw = tn // 2
    for r0 in range(0, tq, ROW_CHUNK):
        f_lo = f_hi = None
        for k in range(TOP_K):
            wu = pltpu.bitcast(ybuf[pl.ds(slot * n_rows + k * tq + r0, ROW_CHUNK), :], jnp.uint32)
            gk = gate_ref[r0:r0 + ROW_CHUNK, k:k + 1]
            lo = gk * pltpu.bitcast(wu << 16, F32)
            hi = gk * pltpu.bitcast(wu & jnp.uint32(0xFFFF0000), F32)
            f_lo = lo if f_lo is None else f_lo + lo
            f_hi = hi if f_hi is None else f_hi + hi
        parts = []
        for c0 in range(0, f_lo.shape[1], hw):
            parts += [f_lo[:, c0:c0 + hw], f_hi[:, c0:c0 + hw]]
        f = jnp.concatenate(parts, axis=-1)
        h = DEEPNORM_ALPHA * x1_ref[r0:r0 + ROW_CHUNK, :] + (1.0 + gate2) * f
        mu = jnp.mean(h, axis=-1, keepdims=True)
        hc = h - mu
        var = jnp.mean(jnp.square(hc), axis=-1, keepdims=True)
        o_ref[r0:r0 + ROW_CHUNK, :] = hc * lax.rsqrt(var + LN_EPS) * g_ref[...] + b_ref[...]


def _combine(pos_steps, y, gates, x1, mod, ln_g, ln_b, seq, tn):
    T, D = x1.shape
    n_steps = pos_steps.shape[0]
    tq = T // n_steps
    steps_per_batch = seq // tq
    row = lambda s: (0, 0)
    return pl.pallas_call(
        functools.partial(_combine_kernel, tn=tn),
        out_shape=jax.ShapeDtypeStruct((T, D), F32),
        grid=(n_steps,),
        in_specs=[pl.BlockSpec(memory_space=pl.ANY),
                  pl.BlockSpec(memory_space=pl.ANY),
                  pl.BlockSpec((tq, TOP_K), lambda s: (s, 0)),
                  pl.BlockSpec((tq, D), lambda s: (s, 0)),
                  pl.BlockSpec((1, 6, D), lambda s: (s // steps_per_batch, 0, 0)),
                  pl.BlockSpec((1, D), row), pl.BlockSpec((1, D), row)],
        out_specs=pl.BlockSpec((tq, D), lambda s: (s, 0)),
        scratch_shapes=[pltpu.SMEM((2 * TOP_K * tq,), I32),
                        pltpu.VMEM((2 * TOP_K * tq, D // 2), I32),
                        pltpu.SemaphoreType.DMA((2,)),
                        pltpu.SemaphoreType.DMA((2,))],
        compiler_params=_cparams(("arbitrary",)),
        name="combine",
    )(pos_steps, y, gates, x1, mod, ln_g.reshape(1, D), ln_b.reshape(1, D))


def kernel(x, c, rel_bias, w_mod, b_mod, w_in, b_in, lambda_q1, lambda_k1, lambda_q2, lambda_k2, subln_g, sinks,
           w_out, b_out, ln1_g, ln1_b, w_router, b_router, w1, b1, w2, b2, ln2_g, ln2_b):
    B, S, D = x.shape
    T = B * S
    E = w_router.shape[2]
    n_heads_diff = (D // 2) // (2 * HEAD_DIM_DIFF)
    n_heads_swa = (D // 2) // HEAD_DIM_SWA
    t_att = min(ATT_TILE, S)
    for l in range(DEPTH):
        lambda_init = 0.8 - 0.6 * math.exp(-0.3 * l)
        mod = _mod(c, w_mod[l], b_mod[l]).reshape(B, 6, D)
        proj = _in_proj(x, mod, w_in[l].astype(BF16), b_in[l])
        bias_d = _bias_tiles(rel_bias, 0, n_heads_diff, 2, t_att, t_att, True)
        bias_s = _bias_tiles(rel_bias, n_heads_diff, n_heads_swa, 2, WINDOW, 2 * WINDOW, False)
        lam_vecs = jnp.stack([lambda_q1[l], lambda_k1[l], lambda_q2[l], lambda_k2[l]]).astype(F32)
        a_diff = _diff_attn(proj, bias_d, lam_vecs, subln_g[l], n_heads_diff, lambda_init)
        a_swa = _swa_attn(proj, bias_s, sinks[l], n_heads_diff, n_heads_swa)
        x1, u2p, logits = _out_proj(a_diff, a_swa, w_out[l].astype(BF16), x, mod, b_out[l], ln1_g[l], ln1_b[l],
                                    w_router[l].astype(BF16), b_router[l])
        top_idx, gates = _router(logits.reshape(T, E))
        tm = MOE_TM
        assert tm % MOE_SUB == 0
        n_tiles = (T * TOP_K) // tm + E
        tq = min(CMB_TQ, S)
        sorted_tok, first_pair, tile_expert, tile_sub, n_used, pos_steps = _routing_tables(
            top_idx, E, tm, MOE_SUB, n_tiles, tq)
        act = _moe_up(tile_expert, tile_sub, first_pair, sorted_tok, u2p.reshape(T, D // 2), w1[l], b1[l], tm)
        y = _moe_down(tile_expert, tile_sub, n_used, act, w2[l], b2[l], tm)
        x = _combine(pos_steps, y, gates, x1.reshape(T, D), mod, ln2_g[l], ln2_b[l], S,
                     min(MOE_TN, D)).reshape(B, S, D)
    return x
```

```python
import functools
import math

import jax
import jax.numpy as jnp
from jax import lax
from jax.experimental import pallas as pl
from jax.experimental.pallas import tpu as pltpu

F32 = jnp.float32
BF16 = jnp.bfloat16
I32 = jnp.int32

HEAD_DIM_DIFF = 128
HEAD_DIM_SWA = 64
SWA_GROUP = 8
WINDOW = 128
N_BUCKETS = 32
MAX_DISTANCE = 128
TOP_K = 4
SWIGLU_ALPHA = 1.702
SWIGLU_LIMIT = 7.0
LN_EPS = 1e-5
RMS_EPS = 1e-5
DEPTH = 1
DEEPNORM_ALPHA = (2.0 * DEPTH) ** 0.25
NEG_BIG = -1e30
LOG2E = 1.4426950408889634

LANES = 128
VMEM_LIMIT = 56 * 1024 * 1024

ATT_TILE = 512
ATT_SLAB = 32
ATT_HEADS = 2
PROJ_TM = 1024
PROJ_TN = 512
OUT_TM = 1024
OUT_TK = 512
ROW_CHUNK = 128
MOE_TM = 2304
MOE_SUB = 256
MOE_TF = 128
MOE_TN = 1024
DMA_UNROLL = 8
CMB_TQ = 256
ROUTER_TR = 2048


def _cparams(sem):
    return pltpu.CompilerParams(dimension_semantics=sem, vmem_limit_bytes=VMEM_LIMIT)


def _mod_kernel(c_ref, w_ref, b_ref, o_ref):
    c = c_ref[...]
    s = c * (1.0 / (1.0 + jnp.exp(-c)))
    acc = jnp.dot(s.astype(BF16), w_ref[...].astype(BF16), preferred_element_type=F32)
    o_ref[...] = acc + b_ref[...]


def _mod(c, w_mod, b_mod, tn=512):
    B, D = c.shape
    N = w_mod.shape[1]
    return pl.pallas_call(
        _mod_kernel,
        out_shape=jax.ShapeDtypeStruct((B, N), F32),
        grid=(N // tn,),
        in_specs=[pl.BlockSpec((B, D), lambda j: (0, 0)),
                  pl.BlockSpec((D, tn), lambda j: (0, j)),
                  pl.BlockSpec((1, tn), lambda j: (0, j))],
        out_specs=pl.BlockSpec((B, tn), lambda j: (0, j)),
        compiler_params=_cparams(("arbitrary",)),
        name="mod",
    )(c, w_mod, b_mod.reshape(1, N))


def _in_proj_kernel(x_hbm, mod_ref, w_ref, b_ref, o_ref, xbuf, u_ref, sem):
    i, j = pl.program_id(1), pl.program_id(2)
    ni = pl.num_programs(1)
    tm = xbuf.shape[0]
    step = pl.program_id(0) * ni + i
    n_steps = pl.num_programs(0) * ni

    def x_copy(s):
        return pltpu.make_async_copy(x_hbm.at[s // ni, pl.ds((s % ni) * tm, tm)], xbuf, sem)

    @pl.when(j == 0)
    def _():
        @pl.when(step == 0)
        def _():
            x_copy(0).start()

        x_copy(step).wait()
        shift = mod_ref[0, 0:1, :]
        scale = mod_ref[0, 1:2, :]
        for r0 in range(0, tm, ROW_CHUNK):
            u_ref[r0:r0 + ROW_CHUNK, :] = (xbuf[r0:r0 + ROW_CHUNK, :] * (1.0 + scale) + shift).astype(BF16)

    @pl.when((j == 1) & (step + 1 < n_steps))
    def _():
        x_copy(step + 1).start()

    acc = jnp.dot(u_ref[...], w_ref[...], preferred_element_type=F32)
    o_ref[0] = (acc + b_ref[...]).astype(BF16)


def _in_proj(x, mod, w_in_bf, b_in):
    B, S, D = x.shape
    N = w_in_bf.shape[1]
    tm, tn = min(PROJ_TM, S), PROJ_TN
    assert N // tn >= 2
    return pl.pallas_call(
        _in_proj_kernel,
        out_shape=jax.ShapeDtypeStruct((B, S, N), BF16),
        grid=(B, S // tm, N // tn),
        in_specs=[pl.BlockSpec(memory_space=pl.ANY),
                  pl.BlockSpec((1, 6, D), lambda b, i, j: (b, 0, 0)),
                  pl.BlockSpec((D, tn), lambda b, i, j: (0, j)),
                  pl.BlockSpec((1, tn), lambda b, i, j: (0, j))],
        out_specs=pl.BlockSpec((1, tm, tn), lambda b, i, j: (b, i, j)),
        scratch_shapes=[pltpu.VMEM((tm, D), F32), pltpu.VMEM((tm, D), BF16), pltpu.SemaphoreType.DMA(())],
        compiler_params=_cparams(("arbitrary", "arbitrary", "arbitrary")),
        name="in_proj",
    )(x, mod, w_in_bf, b_in.reshape(1, N))


def _bias_kernel(table_ref, o_ref, *, col0, causal):
    h = pl.program_id(0) + col0
    v = pl.program_id(1)
    rows, cols = o_ref.shape[2], o_ref.shape[3]
    qi = lax.broadcasted_iota(I32, (rows, cols), 0)
    kj = lax.broadcasted_iota(I32, (rows, cols), 1)
    if causal:
        dist = qi - kj + v * rows
        valid = dist >= 0
    else:
        dist = qi - kj + rows
        valid = (dist >= 0) & (dist < WINDOW) & jnp.logical_not((v == 1) & (kj < rows))
    n = jnp.maximum(dist, 0)
    max_exact = N_BUCKETS // 2
    nf = jnp.maximum(n, 1).astype(F32)
    large = max_exact + (jnp.log(nf / max_exact) / math.log(MAX_DISTANCE / max_exact)
                         * (N_BUCKETS - max_exact)).astype(I32)
    large = jnp.minimum(large, N_BUCKETS - 1)
    bucket = jnp.where(n < max_exact, n, large)
    val = jnp.zeros((rows, cols), F32)
    for bkt in range(N_BUCKETS):
        val = jnp.where(bucket == bkt, table_ref[bkt, h], val)
    if causal:
        val = val - table_ref[N_BUCKETS - 1, h]
    o_ref[0, 0] = jnp.where(valid, val * LOG2E, NEG_BIG)


def _bias_tiles(rel_bias, col0, n_heads, n_var, rows, cols, causal):
    return pl.pallas_call(
        functools.partial(_bias_kernel, col0=col0, causal=causal),
        out_shape=jax.ShapeDtypeStruct((n_heads, n_var, rows, cols), F32),
        grid=(n_heads, n_var),
        in_specs=[pl.BlockSpec(memory_space=pltpu.SMEM)],
        out_specs=pl.BlockSpec((1, 1, rows, cols), lambda h, v: (h, v, 0, 0)),
        compiler_params=_cparams(("arbitrary", "arbitrary")),
        name="bias_tiles",
    )(rel_bias)


def _diff_attn_kernel(q_ref, k_ref, v_ref, bias_ref, lam_ref, g_ref, o_ref,
                      m_ref, l_ref, acc_ref, s_ref, p_ref, a_ref, *, lambda_init):
    t = q_ref.shape[1]
    d = HEAD_DIM_DIFF
    n_ch = 2 * ATT_HEADS
    qi = pl.program_id(2)
    q = (q_ref[0].astype(F32) * (d ** -0.5 * LOG2E)).astype(BF16)

    lv = lam_ref[...]
    lam = (jnp.exp(jnp.sum(lv[0:1] * lv[1:2], axis=-1, keepdims=True))
           - jnp.exp(jnp.sum(lv[2:3] * lv[3:4], axis=-1, keepdims=True)) + lambda_init)

    nt = (((1,), (1,)), ((), ()))
    m_ref[...] = jnp.full(m_ref.shape, NEG_BIG, F32)
    l_ref[...] = jnp.zeros(l_ref.shape, F32)
    acc_ref[...] = jnp.zeros(acc_ref.shape, F32)

    def block(kj, var):
        k = k_ref[0, pl.ds(pl.multiple_of(kj * t, t), t), :]
        v = v_ref[0, pl.ds(pl.multiple_of(kj * t, t), t), :]
        for c in range(n_ch):
            s_ref[c] = lax.dot_general(q[:, c * d:(c + 1) * d], k[:, c * d:(c + 1) * d], nt,
                                       preferred_element_type=F32)
        for c in range(n_ch):
            for r0 in range(0, t, ATT_SLAB):
                rs = slice(r0, r0 + ATT_SLAB)
                ch = []
                for j0 in range(0, t, LANES):
                    sj = s_ref[c, rs, j0:j0 + LANES]
                    if var is not None:
                        sj = sj + bias_ref[c // 2, var, rs, j0:j0 + LANES]
                    ch.append(sj)
                m_old = m_ref[c, rs, :]
                m_new = jnp.maximum(m_old, jnp.max(functools.reduce(jnp.maximum, ch), axis=-1, keepdims=True))
                alpha = jnp.exp2(m_old - m_new)
                ps = [jnp.exp2(sj - m_new) for sj in ch]
                l_ref[c, rs, :] = alpha * l_ref[c, rs, :] + functools.reduce(jnp.add, ps)
                m_ref[c, rs, :] = m_new
                a_ref[c, rs, :] = alpha
                for j, pj in enumerate(ps):
                    p_ref[c, rs, j * LANES:(j + 1) * LANES] = pj.astype(BF16)
            al = a_ref[c]
            hh = c // 2
            acc_ref[c] = (acc_ref[c] * jnp.concatenate([al] * (2 * d // LANES), axis=-1)
                          + jnp.dot(p_ref[c], v[:, hh * 2 * d:(hh + 1) * 2 * d], preferred_element_type=F32))

    def far_step(kj, carry):
        block(kj, None)
        return carry

    def near_step(kj, carry):
        block(kj, qi - kj)
        return carry

    n_far = jnp.maximum(qi - 1, 0)
    lax.fori_loop(0, n_far, far_step, 0)
    lax.fori_loop(n_far, qi + 1, near_step, 0)

    for hh in range(ATT_HEADS):
        l0 = jnp.sum(l_ref[2 * hh], axis=-1, keepdims=True)
        l1 = jnp.sum(l_ref[2 * hh + 1], axis=-1, keepdims=True)
        o = acc_ref[2 * hh] / l0 - lam * (acc_ref[2 * hh + 1] / l1)
        o = o * lax.rsqrt(jnp.mean(jnp.square(o), axis=-1, keepdims=True) + RMS_EPS) * g_ref[...]
        o_ref[0, :, hh * 2 * d:(hh + 1) * 2 * d] = (o * (1.0 - lambda_init)).astype(BF16)


def _diff_attn(proj, bias, lam_vecs, subln_g, n_heads, lambda_init):
    B, S, _ = proj.shape
    t = min(ATT_TILE, S)
    w = 2 * HEAD_DIM_DIFF
    nh = ATT_HEADS
    assert n_heads % nh == 0
    ng = n_heads // nh
    return pl.pallas_call(
        functools.partial(_diff_attn_kernel, lambda_init=lambda_init),
        out_shape=jax.ShapeDtypeStruct((B, S, n_heads * w), BF16),
        grid=(B, ng, S // t),
        in_specs=[pl.BlockSpec((1, t, nh * w), lambda b, h, i: (b, i, h)),
                  pl.BlockSpec((1, S, nh * w), lambda b, h, i: (b, 0, ng + h)),
                  pl.BlockSpec((1, S, nh * w), lambda b, h, i: (b, 0, 2 * ng + h)),
                  pl.BlockSpec((nh, 2, t, t), lambda b, h, i: (h, 0, 0, 0)),
                  pl.BlockSpec((4, HEAD_DIM_DIFF), lambda b, h, i: (0, 0)),
                  pl.BlockSpec((1, w), lambda b, h, i: (0, 0))],
        out_specs=pl.BlockSpec((1, t, nh * w), lambda b, h, i: (b, i, h)),
        scratch_shapes=[pltpu.VMEM((2 * nh, t, LANES), F32), pltpu.VMEM((2 * nh, t, LANES), F32),
                        pltpu.VMEM((2 * nh, t, w), F32), pltpu.VMEM((2 * nh, t, t), F32),
                        pltpu.VMEM((2 * nh, t, t), BF16), pltpu.VMEM((2 * nh, t, LANES), F32)],
        compiler_params=_cparams(("arbitrary", "arbitrary", "arbitrary")),
        name="diff_attn",
    )(proj, proj, proj, bias, lam_vecs, subln_g.reshape(1, w))


def _swa_kernel(sink_ref, q_ref, kc_ref, vc_ref, kp_ref, vp_ref, bias_ref, o_ref, *, n_kv):
    d = HEAD_DIM_SWA
    var = jnp.where(pl.program_id(1) == 0, 1, 0)
    q = (q_ref[0].astype(F32) * (d ** -0.5 * LOG2E)).astype(BF16)
    kk = jnp.concatenate([kp_ref[0], kc_ref[0]], axis=0)
    vv = jnp.concatenate([vp_ref[0], vc_ref[0]], axis=0)
    lo_half = lax.broadcasted_iota(I32, (kk.shape[0], LANES), 1) < d
    zero = jnp.zeros((kk.shape[0], LANES), BF16)
    nt = (((1,), (1,)), ((), ()))

    def halves(x2, e):
        swapped = jnp.concatenate([x2[:, d:], x2[:, :d]], axis=1)
        own_lo = x2 if e == 0 else swapped
        own_hi = swapped if e == 0 else x2
        return jnp.where(lo_half, own_lo, zero), jnp.where(lo_half, zero, own_hi)

    for h in range(n_kv):
        pr, e = divmod(h, 2)
        k_lo, k_hi = halves(kk[:, pr * LANES:(pr + 1) * LANES], e)
        v_lo, v_hi = halves(vv[:, pr * LANES:(pr + 1) * LANES], e)
        heads = [h * SWA_GROUP + g for g in range(SWA_GROUP)]
        sinks = [sink_ref[hq] * LOG2E for hq in heads]
        s = [lax.dot_general(q[:, (hq // 2) * LANES:(hq // 2 + 1) * LANES], k_lo if hq % 2 == 0 else k_hi, nt,
                             preferred_element_type=F32) + bias_ref[hq, var] for hq in heads]
        m = [jnp.maximum(jnp.max(s_g, axis=-1, keepdims=True), sk) for s_g, sk in zip(s, sinks)]
        p = [jnp.exp2(s_g - m_g) for s_g, m_g in zip(s, m)]
        denom = [jnp.sum(p_g, axis=-1, keepdims=True) + jnp.exp2(sk - m_g) for p_g, m_g, sk in zip(p, m, sinks)]
        pn = [(p_g * (1.0 / d_g)).astype(BF16) for p_g, d_g in zip(p, denom)]
        for gp in range(SWA_GROUP // 2):
            col = (heads[2 * gp] // 2) * LANES
            o = (jnp.dot(pn[2 * gp], v_lo, preferred_element_type=F32)
                 + jnp.dot(pn[2 * gp + 1], v_hi, preferred_element_type=F32))
            o_ref[0, :, col:col + LANES] = o.astype(BF16)


def _swa_attn(proj, bias, sinks, n_heads_diff, n_heads_swa):
    B, S, _ = proj.shape
    n_kv = n_heads_swa // SWA_GROUP
    r = WINDOW
    qw = n_heads_swa * HEAD_DIM_SWA
    kw = n_kv * HEAD_DIM_SWA
    q_blk = (3 * n_heads_diff * 2 * HEAD_DIM_DIFF) // qw
    k_blk = (3 * n_heads_diff * 2 * HEAD_DIM_DIFF + qw) // kw
    return pl.pallas_call(
        functools.partial(_swa_kernel, n_kv=n_kv),
        out_shape=jax.ShapeDtypeStruct((B, S, qw), BF16),
        grid=(B, S // r),
        in_specs=[pl.BlockSpec(memory_space=pltpu.SMEM),
                  pl.BlockSpec((1, r, qw), lambda b, i: (b, i, q_blk)),
                  pl.BlockSpec((1, r, kw), lambda b, i: (b, i, k_blk)),
                  pl.BlockSpec((1, r, kw), lambda b, i: (b, i, k_blk + 1)),
                  pl.BlockSpec((1, r, kw), lambda b, i: (b, jnp.maximum(i - 1, 0), k_blk)),
                  pl.BlockSpec((1, r, kw), lambda b, i: (b, jnp.maximum(i - 1, 0), k_blk + 1)),
                  pl.BlockSpec((n_heads_swa, 2, r, 2 * r), lambda b, i: (0, 0, 0, 0))],
        out_specs=pl.BlockSpec((1, r, qw), lambda b, i: (b, i, 0)),
        compiler_params=_cparams(("arbitrary", "arbitrary")),
        name="swa_attn",
    )(sinks, proj, proj, proj, proj, proj, bias)


def _pack_bf16_pair(lo, hi):
    lo_b = pltpu.bitcast(lo.astype(BF16).astype(F32), jnp.uint32) >> 16
    hi_b = pltpu.bitcast(hi.astype(BF16).astype(F32), jnp.uint32) & jnp.uint32(0xFFFF0000)
    return pltpu.bitcast(lo_b | hi_b, I32)


def _unpack_bf16_pair(w):
    wu = pltpu.bitcast(w, jnp.uint32)
    lo = pltpu.bitcast(wu << 16, F32).astype(BF16)
    hi = pltpu.bitcast(wu & jnp.uint32(0xFFFF0000), F32).astype(BF16)
    return lo, hi


def _out_proj_kernel(ad_ref, as_ref, w_ref, x_hbm, mod_ref, bo_ref, g_ref, b_ref, wr_ref, br_ref,
                     x1_hbm, u2_hbm, lg_ref, acc_ref, xbuf, x1buf, u2buf, x_sem, o_sem, u_sem, *, n_kd):
    b, i, k = pl.program_id(0), pl.program_id(1), pl.program_id(2)
    nk = pl.num_programs(2)
    tm = acc_ref.shape[0]
    n_chunks = tm // ROW_CHUNK

    def rows(c):
        return pl.ds(i * tm + c * ROW_CHUNK, ROW_CHUNK)

    def x_copy(c):
        return pltpu.make_async_copy(x_hbm.at[b, rows(c)], xbuf.at[c % 2], x_sem.at[c % 2])

    def o_copy(c):
        return pltpu.make_async_copy(x1buf.at[c % 2], x1_hbm.at[b, rows(c)], o_sem.at[c % 2])

    def u_copy(c):
        return pltpu.make_async_copy(u2buf.at[c % 2], u2_hbm.at[b, rows(c)], u_sem.at[c % 2])

    @pl.when(k == 0)
    def _():
        x_copy(0).start()
        acc_ref[...] = jnp.dot(ad_ref[0], w_ref[...], preferred_element_type=F32)

    @pl.when((k > 0) & (k < n_kd))
    def _():
        acc_ref[...] += jnp.dot(ad_ref[0], w_ref[...], preferred_element_type=F32)

    @pl.when(k >= n_kd)
    def _():
        acc_ref[...] += jnp.dot(as_ref[0], w_ref[...], preferred_element_type=F32)

    @pl.when(k == nk - 1)
    def _():
        gate1 = mod_ref[0, 2:3, :]
        shift2 = mod_ref[0, 3:4, :]
        scale2 = mod_ref[0, 4:5, :]
        half = acc_ref.shape[1] // 2
        for c in range(n_chunks):
            rs = slice(c * ROW_CHUNK, (c + 1) * ROW_CHUNK)
            x_copy(c).wait()
            if c + 1 < n_chunks:
                x_copy(c + 1).start()
            if c >= 2:
                o_copy(c - 2).wait()
                u_copy(c - 2).wait()
            a = acc_ref[rs, :] + bo_ref[...]
            h = DEEPNORM_ALPHA * xbuf[c % 2] + (1.0 + gate1) * a
            mu = jnp.mean(h, axis=-1, keepdims=True)
            hc = h - mu
            var = jnp.mean(jnp.square(hc), axis=-1, keepdims=True)
            x1 = hc * lax.rsqrt(var + LN_EPS) * g_ref[...] + b_ref[...]
            x1buf[c % 2] = x1
            u2 = x1 * (1.0 + scale2) + shift2
            lg_ref[0, rs, :] = jnp.dot(u2.astype(BF16), wr_ref[...], preferred_element_type=F32) + br_ref[...]
            u2buf[c % 2] = _pack_bf16_pair(u2[:, :half], u2[:, half:])
            o_copy(c).start()
            u_copy(c).start()
        for c in range(max(n_chunks - 2, 0), n_chunks):
            o_copy(c).wait()
            u_copy(c).wait()


def _out_proj(a_diff, a_swa, w_out_bf, x, mod, b_out, ln_g, ln_b, w_router_bf, b_router):
    B, S, D = x.shape
    E = w_router_bf.shape[1]
    tm, tk = min(OUT_TM, S), OUT_TK
    n_kd = a_diff.shape[2] // tk
    n_ks = a_swa.shape[2] // tk
    row = lambda b, i, k: (0, 0)
    return pl.pallas_call(
        functools.partial(_out_proj_kernel, n_kd=n_kd),
        out_shape=(jax.ShapeDtypeStruct((B, S, D), F32),
                   jax.ShapeDtypeStruct((B, S, D // 2), I32),
                   jax.ShapeDtypeStruct((B, S, E), F32)),
        grid=(B, S // tm, n_kd + n_ks),
        in_specs=[pl.BlockSpec((1, tm, tk), lambda b, i, k: (b, i, jnp.minimum(k, n_kd - 1))),
                  pl.BlockSpec((1, tm, tk), lambda b, i, k: (b, i, jnp.maximum(k - n_kd, 0))),
                  pl.BlockSpec((tk, D), lambda b, i, k: (k, 0)),
                  pl.BlockSpec(memory_space=pl.ANY),
                  pl.BlockSpec((1, 6, D), lambda b, i, k: (b, 0, 0)),
                  pl.BlockSpec((1, D), row), pl.BlockSpec((1, D), row), pl.BlockSpec((1, D), row),
                  pl.BlockSpec((D, E), row), pl.BlockSpec((1, E), row)],
        out_specs=(pl.BlockSpec(memory_space=pl.ANY),
                   pl.BlockSpec(memory_space=pl.ANY),
                   pl.BlockSpec((1, tm, E), lambda b, i, k: (b, i, 0))),
        scratch_shapes=[pltpu.VMEM((tm, D), F32),
                        pltpu.VMEM((2, ROW_CHUNK, D), F32), pltpu.VMEM((2, ROW_CHUNK, D), F32),
                        pltpu.VMEM((2, ROW_CHUNK, D // 2), I32),
                        pltpu.SemaphoreType.DMA((2,)), pltpu.SemaphoreType.DMA((2,)),
                        pltpu.SemaphoreType.DMA((2,))],
        compiler_params=_cparams(("arbitrary", "arbitrary", "arbitrary")),
        name="out_proj",
    )(a_diff, a_swa, w_out_bf, x, mod, b_out.reshape(1, D), ln_g.reshape(1, D), ln_b.reshape(1, D),
      w_router_bf, b_router.reshape(1, E))


def _router_kernel(lg_ref, idx_ref, gate_ref):
    l = lg_ref[...]
    n, e = l.shape
    eidx = lax.broadcasted_iota(I32, (n, e), 1)
    kcol = lax.broadcasted_iota(I32, (n, TOP_K), 1)
    idx_out = jnp.zeros((n, TOP_K), I32)
    val_out = jnp.zeros((n, TOP_K), F32)
    for r in range(TOP_K):
        m = jnp.max(l, axis=-1, keepdims=True)
        i = jnp.min(jnp.where(l == m, eidx, e), axis=-1, keepdims=True)
        idx_out = jnp.where(kcol == r, i, idx_out)
        val_out = jnp.where(kcol == r, m, val_out)
        l = jnp.where(eidx == i, -jnp.inf, l)
    ex = jnp.exp(val_out - jnp.max(val_out, axis=-1, keepdims=True))
    idx_ref[...] = idx_out
    gate_ref[...] = ex / jnp.sum(ex, axis=-1, keepdims=True)


def _router(logits):
    T, E = logits.shape
    tr = min(ROUTER_TR, T)
    return pl.pallas_call(
        _router_kernel,
        out_shape=(jax.ShapeDtypeStruct((T, TOP_K), I32), jax.ShapeDtypeStruct((T, TOP_K), F32)),
        grid=(T // tr,),
        in_specs=[pl.BlockSpec((tr, E), lambda i: (i, 0))],
        out_specs=(pl.BlockSpec((tr, TOP_K), lambda i: (i, 0)), pl.BlockSpec((tr, TOP_K), lambda i: (i, 0))),
        compiler_params=_cparams(("arbitrary",)),
        name="router",
    )(logits)


def _routing_tables(top_idx, n_experts, tm, sub, n_tiles, tq):
    T = top_idx.shape[0]
    flat_e = top_idx.reshape(-1)
    onehot = (flat_e[:, None] == jnp.arange(n_experts, dtype=I32)[None, :]).astype(I32)
    csum = jnp.cumsum(onehot, axis=0)
    rank = jnp.sum(csum * onehot, axis=1) - 1
    counts = csum[-1]
    tiles_per = (counts + tm - 1) // tm
    tile_end = jnp.cumsum(tiles_per)
    tile_start = tile_end - tiles_per
    n_used = tile_end[-1]
    pos = tile_start[flat_e] * tm + rank
    tile_ids = jnp.arange(n_tiles, dtype=I32)
    tile_expert = jnp.sum((tile_ids[:, None] >= tile_end[None, :]).astype(I32), axis=1)
    last_expert = jnp.max(jnp.where(counts > 0, jnp.arange(n_experts, dtype=I32), 0))
    tile_expert = jnp.where(tile_ids < n_used, tile_expert, last_expert).astype(I32)
    first_in_expert = (tile_ids - tile_start[tile_expert]) * tm
    tile_rows = jnp.where(tile_ids < n_used, jnp.clip(counts[tile_expert] - first_in_expert, 0, tm), 0)
    tile_sub = ((tile_rows + sub - 1) // sub).astype(I32)
    first_pair = (jnp.cumsum(counts) - counts)[tile_expert] + first_in_expert
    first_pair = jnp.clip(first_pair, 0, T * TOP_K).astype(I32)
    order = jnp.argsort(flat_e, stable=True).astype(I32)
    sorted_tok = jnp.concatenate([order // TOP_K, jnp.zeros((_idx_len(tm),), I32)])
    pos_steps = pos.reshape(T // tq, tq, TOP_K).transpose(0, 2, 1).reshape(T // tq, TOP_K * tq)
    return sorted_tok, first_pair, tile_expert, tile_sub, n_used.reshape(1).astype(I32), pos_steps.astype(I32)


def _sub_block_loops(nsub, n_sub_max, sub, compute, zero):
    def pair_body(s, carry):
        compute(pl.multiple_of(s * 2 * sub, 2 * sub), 2 * sub)
        return carry

    lax.fori_loop(0, nsub // 2, pair_body, 0)

    @pl.when(nsub % 2 == 1)
    def _():
        compute(pl.multiple_of((nsub - 1) * sub, sub), sub)

    def zero_body(s, carry):
        zero(pl.multiple_of(s * sub, sub), sub)
        return carry

    lax.fori_loop(nsub, n_sub_max, zero_body, 0)


def _idx_len(tm):
    return -(-(tm + LANES) // 1024) * 1024


def _moe_up_kernel(te_ref, ts_ref, fp_ref, tok_hbm, u2_hbm, wg_ref, wl_ref, bg_ref, bl_ref, o_ref,
                   idx_smem, gbuf, xs_ref, wcat_ref, idx_sem, row_sem):
    t = pl.program_id(0)
    j = pl.program_id(1)
    n_tiles = pl.num_programs(0)
    tm = xs_ref.shape[0]
    half = gbuf.shape[1]
    tf = o_ref.shape[1]
    nsub = ts_ref[t]
    il = _idx_len(tm)

    def idx_copy(tile, slot):
        start = pl.multiple_of((fp_ref[tile] // LANES) * LANES, LANES)
        return pltpu.make_async_copy(tok_hbm.at[pl.ds(start, il)], idx_smem.at[pl.ds(slot * il, il)],
                                     idx_sem.at[slot])

    def issue_rows(tile, slot):
        first = slot * il + fp_ref[tile] % LANES

        def body(g, carry):
            base = pl.multiple_of(g * DMA_UNROLL, DMA_UNROLL)
            for u in range(DMA_UNROLL):
                tok = idx_smem[first + base + u]
                pltpu.make_async_copy(u2_hbm.at[pl.ds(tok, 1)], gbuf.at[pl.ds(base + u, 1)], row_sem).start()
            return carry
        lax.fori_loop(0, ts_ref[tile] * (MOE_SUB // DMA_UNROLL), body, 0)

    def wait_sub_block(s, carry):
        pltpu.make_async_copy(u2_hbm.at[pl.ds(0, MOE_SUB)], gbuf.at[pl.ds(0, MOE_SUB)], row_sem).wait()
        return carry

    def unpack_sub_block(s, carry):
        r0 = pl.multiple_of(s * MOE_SUB, MOE_SUB)
        lo, hi = _unpack_bf16_pair(gbuf[pl.ds(r0, MOE_SUB), :])
        xs_ref[pl.ds(r0, MOE_SUB), :half] = lo
        xs_ref[pl.ds(r0, MOE_SUB), half:] = hi
        return carry

    @pl.when(j == 0)
    def _():
        slot = t % 2

        @pl.when(t == 0)
        def _():
            idx_copy(0, 0).start()
            idx_copy(0, 0).wait()
            issue_rows(0, 0)

            @pl.when(n_tiles > 1)
            def _():
                idx_copy(1, 1).start()

        lax.fori_loop(0, nsub, wait_sub_block, 0)
        lax.fori_loop(0, nsub, unpack_sub_block, 0)

        @pl.when(t + 1 < n_tiles)
        def _():
            idx_copy(t + 1, 1 - slot).wait()
            issue_rows(t + 1, 1 - slot)

        @pl.when(t + 2 < n_tiles)
        def _():
            idx_copy(t + 2, slot).start()

    @pl.when(nsub > 0)
    def _():
        wcat_ref[:, :tf] = wg_ref[0].astype(BF16)
        wcat_ref[:, tf:] = wl_ref[0].astype(BF16)

    def compute(r0, rows):
        h = jnp.dot(xs_ref[pl.ds(r0, rows), :], wcat_ref[...], preferred_element_type=F32)
        glu = jnp.minimum(h[:, :tf] + bg_ref[0], SWIGLU_LIMIT)
        lin = jnp.clip(h[:, tf:] + bl_ref[0], -SWIGLU_LIMIT, SWIGLU_LIMIT)
        act = glu * (1.0 / (1.0 + jnp.exp(-SWIGLU_ALPHA * glu))) * (lin + 1.0)
        o_ref[pl.ds(r0, rows), :] = act.astype(BF16)

    def zero(r0, rows):
        o_ref[pl.ds(r0, rows), :] = jnp.zeros((rows, tf), BF16)

    _sub_block_loops(nsub, tm // MOE_SUB, MOE_SUB, compute, zero)


def _moe_up(tile_expert, tile_sub, first_pair, sorted_tok, u2p, w1, b1, tm):
    E, D, F2 = w1.shape
    F = F2 // 2
    n_tiles = tile_expert.shape[0]
    tf = min(MOE_TF, F)
    nj = F // tf

    def jj(t, j, ts):
        return jnp.where(ts[t] > 0, j, nj - 1)

    grid_spec = pltpu.PrefetchScalarGridSpec(
        num_scalar_prefetch=3,
        grid=(n_tiles, nj),
        in_specs=[pl.BlockSpec(memory_space=pl.ANY),
                  pl.BlockSpec(memory_space=pl.ANY),
                  pl.BlockSpec((1, D, tf), lambda t, j, te, ts, fp: (te[t], 0, jj(t, j, ts))),
                  pl.BlockSpec((1, D, tf), lambda t, j, te, ts, fp: (te[t], 0, nj + jj(t, j, ts))),
                  pl.BlockSpec((1, 1, tf), lambda t, j, te, ts, fp: (te[t], 0, jj(t, j, ts))),
                  pl.BlockSpec((1, 1, tf), lambda t, j, te, ts, fp: (te[t], 0, nj + jj(t, j, ts)))],
        out_specs=pl.BlockSpec((tm, tf), lambda t, j, te, ts, fp: (t, j)),
        scratch_shapes=[pltpu.SMEM((2 * _idx_len(tm),), I32),
                        pltpu.VMEM((tm, D // 2), I32),
                        pltpu.VMEM((tm, D), BF16),
                        pltpu.VMEM((D, 2 * tf), BF16),
                        pltpu.SemaphoreType.DMA((2,)),
                        pltpu.SemaphoreType.DMA(())],
    )
    return pl.pallas_call(
        _moe_up_kernel,
        out_shape=jax.ShapeDtypeStruct((n_tiles * tm, F), BF16),
        grid_spec=grid_spec,
        compiler_params=_cparams(("arbitrary", "arbitrary")),
        name="moe_up",
    )(tile_expert, tile_sub, first_pair, sorted_tok, u2p, w1, w1, b1.reshape(E, 1, F2), b1.reshape(E, 1, F2))


def _moe_down_kernel(te_ref, ts_ref, nu_ref, a_ref, w_ref, b_ref, o_ref, wb_ref):
    nsub = ts_ref[pl.program_id(0)]
    hw = o_ref.shape[1]

    @pl.when(nsub > 0)
    def _():
        wb_ref[...] = w_ref[0].astype(BF16)

    def compute(r0, rows):
        y = jnp.dot(a_ref[pl.ds(r0, rows), :], wb_ref[...], preferred_element_type=F32) + b_ref[0]
        o_ref[pl.ds(r0, rows), :] = _pack_bf16_pair(y[:, :hw], y[:, hw:])

    def zero(r0, rows):
        o_ref[pl.ds(r0, rows), :] = jnp.zeros((rows, hw), I32)

    _sub_block_loops(nsub, a_ref.shape[0] // MOE_SUB, MOE_SUB, compute, zero)


def _moe_down(tile_expert, tile_sub, n_used, act, w2, b2, tm):
    E, F, D = w2.shape
    rows = act.shape[0]
    n_tiles = rows // tm
    tn = min(MOE_TN, D)
    nj = D // tn

    def jj(t, j, ts):
        return jnp.where(ts[t] > 0, j, nj - 1)

    def tt(t, nu):
        return jnp.minimum(t, nu[0] - 1)

    grid_spec = pltpu.PrefetchScalarGridSpec(
        num_scalar_prefetch=3,
        grid=(n_tiles, nj),
        in_specs=[pl.BlockSpec((tm, F), lambda t, j, te, ts, nu: (tt(t, nu), 0)),
                  pl.BlockSpec((1, F, tn), lambda t, j, te, ts, nu: (te[t], 0, jj(t, j, ts))),
                  pl.BlockSpec((1, 1, tn), lambda t, j, te, ts, nu: (te[t], 0, jj(t, j, ts)))],
        out_specs=pl.BlockSpec((tm, tn // 2), lambda t, j, te, ts, nu: (t, j)),
        scratch_shapes=[pltpu.VMEM((F, tn), BF16)],
    )
    return pl.pallas_call(
        _moe_down_kernel,
        out_shape=jax.ShapeDtypeStruct((rows, D // 2), I32),
        grid_spec=grid_spec,
        compiler_params=_cparams(("arbitrary", "arbitrary")),
        name="moe_down",
    )(tile_expert, tile_sub, n_used, act, w2, b2.reshape(E, 1, D))


def _combine_kernel(pos_hbm, y_hbm, gate_ref, x1_ref, mod_ref, g_ref, b_ref, o_ref,
                    idx_smem, ybuf, idx_sem, row_sem, *, tn):
    s = pl.program_id(0)
    n_steps = pl.num_programs(0)
    tq = x1_ref.shape[0]
    n_rows = TOP_K * tq
    slot = s % 2
    nxt = 1 - slot

    def idx_copy(step, sl):
        return pltpu.make_async_copy(pos_hbm.at[step], idx_smem.at[pl.ds(sl * n_rows, n_rows)], idx_sem.at[sl])

    def issue_rows(sl):
        def body(g, carry):
            base = pl.multiple_of(g * DMA_UNROLL, DMA_UNROLL)
            for u in range(DMA_UNROLL):
                row = idx_smem[sl * n_rows + base + u]
                pltpu.make_async_copy(y_hbm.at[pl.ds(row, 1)], ybuf.at[pl.ds(sl * n_rows + base + u, 1)],
                                      row_sem.at[sl]).start()
            return carry
        lax.fori_loop(0, n_rows // DMA_UNROLL, body, 0)

    @pl.when(s == 0)
    def _():
        idx_copy(0, 0).start()
        idx_copy(0, 0).wait()
        issue_rows(0)

        @pl.when(n_steps > 1)
        def _():
            idx_copy(1, 1).start()

    @pl.when(s + 1 < n_steps)
    def _():
        idx_copy(s + 1, nxt).wait()
        issue_rows(nxt)

    pltpu.make_async_copy(y_hbm.at[pl.ds(0, n_rows)], ybuf.at[pl.ds(slot * n_rows, n_rows)],
                          row_sem.at[slot]).wait()

    @pl.when(s + 2 < n_steps)
    def _():
        idx_copy(s + 2, slot).start()

    gate2 = mod_ref[0, 5:6, :]
    hw = tn // 2
    for r0 in range(0, tq, ROW_CHUNK):
        f_lo = f_hi = None
        for k in range(TOP_K):
            wu = pltpu.bitcast(ybuf[pl.ds(slot * n_rows + k * tq + r0, ROW_CHUNK), :], jnp.uint32)
            gk = gate_ref[r0:r0 + ROW_CHUNK, k:k + 1]
            lo = gk * pltpu.bitcast(wu << 16, F32)
            hi = gk * pltpu.bitcast(wu & jnp.uint32(0xFFFF0000), F32)
            f_lo = lo if f_lo is None else f_lo + lo
            f_hi = hi if f_hi is None else f_hi + hi
        parts = []
        for c0 in range(0, f_lo.shape[1], hw):
            parts += [f_lo[:, c0:c0 + hw], f_hi[:, c0:c0 + hw]]
        f = jnp.concatenate(parts, axis=-1)
        h = DEEPNORM_ALPHA * x1_ref[r0:r0 + ROW_CHUNK, :] + (1.0 + gate2) * f
        mu = jnp.mean(h, axis=-1, keepdims=True)
        hc = h - mu
        var = jnp.mean(jnp.square(hc), axis=-1, keepdims=True)
        o_ref[r0:r0 + ROW_CHUNK, :] = hc * lax.rsqrt(var + LN_EPS) * g_ref[...] + b_ref[...]


def _combine(pos_steps, y, gates, x1, mod, ln_g, ln_b, seq, tn):
    T, D = x1.shape
    n_steps = pos_steps.shape[0]
    tq = T // n_steps
    steps_per_batch = seq // tq
    row = lambda s: (0, 0)
    return pl.pallas_call(
        functools.partial(_combine_kernel, tn=tn),
        out_shape=jax.ShapeDtypeStruct((T, D), F32),
        grid=(n_steps,),
        in_specs=[pl.BlockSpec(memory_space=pl.ANY),
                  pl.BlockSpec(memory_space=pl.ANY),
                  pl.BlockSpec((tq, TOP_K), lambda s: (s, 0)),
                  pl.BlockSpec((tq, D), lambda s: (s, 0)),
                  pl.BlockSpec((1, 6, D), lambda s: (s // steps_per_batch, 0, 0)),
                  pl.BlockSpec((1, D), row), pl.BlockSpec((1, D), row)],
        out_specs=pl.BlockSpec((tq, D), lambda s: (s, 0)),
        scratch_shapes=[pltpu.SMEM((2 * TOP_K * tq,), I32),
                        pltpu.VMEM((2 * TOP_K * tq, D // 2), I32),
                        pltpu.SemaphoreType.DMA((2,)),
                        pltpu.SemaphoreType.DMA((2,))],
        compiler_params=_cparams(("arbitrary",)),
        name="combine",
    )(pos_steps, y, gates, x1, mod, ln_g.reshape(1, D), ln_b.reshape(1, D))


def kernel(x, c, rel_bias, w_mod, b_mod, w_in, b_in, lambda_q1, lambda_k1, lambda_q2, lambda_k2, subln_g, sinks,
           w_out, b_out, ln1_g, ln1_b, w_router, b_router, w1, b1, w2, b2, ln2_g, ln2_b):
    B, S, D = x.shape
    T = B * S
    E = w_router.shape[2]
    n_heads_diff = (D // 2) // (2 * HEAD_DIM_DIFF)
    n_heads_swa = (D // 2) // HEAD_DIM_SWA
    t_att = min(ATT_TILE, S)
    for l in range(DEPTH):
        lambda_init = 0.8 - 0.6 * math.exp(-0.3 * l)
        mod = _mod(c, w_mod[l], b_mod[l]).reshape(B, 6, D)
        proj = _in_proj(x, mod, w_in[l].astype(BF16), b_in[l])
        bias_d = _bias_tiles(rel_bias, 0, n_heads_diff, 2, t_att, t_att, True)
        bias_s = _bias_tiles(rel_bias, n_heads_diff, n_heads_swa, 2, WINDOW, 2 * WINDOW, False)
        lam_vecs = jnp.stack([lambda_q1[l], lambda_k1[l], lambda_q2[l], lambda_k2[l]]).astype(F32)
        a_diff = _diff_attn(proj, bias_d, lam_vecs, subln_g[l], n_heads_diff, lambda_init)
        a_swa = _swa_attn(proj, bias_s, sinks[l], n_heads_diff, n_heads_swa)
        x1, u2p, logits = _out_proj(a_diff, a_swa, w_out[l].astype(BF16), x, mod, b_out[l], ln1_g[l], ln1_b[l],
                                    w_router[l].astype(BF16), b_router[l])
        top_idx, gates = _router(logits.reshape(T, E))
        tm = MOE_TM
        assert tm % MOE_SUB == 0
        n_tiles = (T * TOP_K) // tm + E
        tq = min(CMB_TQ, S)
        sorted_tok, first_pair, tile_expert, tile_sub, n_used, pos_steps = _routing_tables(
            top_idx, E, tm, MOE_SUB, n_tiles, tq)
        act = _moe_up(tile_expert, tile_sub, first_pair, sorted_tok, u2p.reshape(T, D // 2), w1[l], b1[l], tm)
        y = _moe_down(tile_expert, tile_sub, n_used, act, w2[l], b2[l], tm)
        x = _combine(pos_steps, y, gates, x1.reshape(T, D), mod, ln2_g[l], ln2_b[l], S,
                     min(MOE_TN, D)).reshape(B, S, D)
    return x
```

```python
import functools
import math

import jax
import jax.numpy as jnp
from jax import lax
from jax.experimental import pallas as pl
from jax.experimental.pallas import tpu as pltpu

F32 = jnp.float32
BF16 = jnp.bfloat16
I32 = jnp.int32

HEAD_DIM_DIFF = 128
HEAD_DIM_SWA = 64
SWA_GROUP = 8
WINDOW = 128
N_BUCKETS = 32
MAX_DISTANCE = 128
TOP_K = 4
SWIGLU_ALPHA = 1.702
SWIGLU_LIMIT = 7.0
LN_EPS = 1e-5
RMS_EPS = 1e-5
DEPTH = 1
DEEPNORM_ALPHA = (2.0 * DEPTH) ** 0.25
NEG_BIG = -1e30
LOG2E = 1.4426950408889634

LANES = 128
VMEM_LIMIT = 56 * 1024 * 1024

ATT_TILE = 512
ATT_SLAB = 32
ATT_HEADS = 2
PROJ_TM = 1024
PROJ_TN = 512
OUT_TM = 1024
OUT_TK = 512
ROW_CHUNK = 128
MOE_TM = 2304
MOE_SUB = 256
MOE_TF = 128
MOE_TN = 1024
DMA_UNROLL = 8
CMB_TQ = 256
ROUTER_TR = 2048


def _cparams(sem):
    return pltpu.CompilerParams(dimension_semantics=sem, vmem_limit_bytes=VMEM_LIMIT)


def _mod_kernel(c_ref, w_ref, b_ref, o_ref):
    c = c_ref[...]
    s = c * (1.0 / (1.0 + jnp.exp(-c)))
    acc = jnp.dot(s.astype(BF16), w_ref[...].astype(BF16), preferred_element_type=F32)
    o_ref[...] = acc + b_ref[...]


def _mod(c, w_mod, b_mod, tn=512):
    B, D = c.shape
    N = w_mod.shape[1]
    return pl.pallas_call(
        _mod_kernel,
        out_shape=jax.ShapeDtypeStruct((B, N), F32),
        grid=(N // tn,),
        in_specs=[pl.BlockSpec((B, D), lambda j: (0, 0)),
                  pl.BlockSpec((D, tn), lambda j: (0, j)),
                  pl.BlockSpec((1, tn), lambda j: (0, j))],
        out_specs=pl.BlockSpec((B, tn), lambda j: (0, j)),
        compiler_params=_cparams(("arbitrary",)),
        name="mod",
    )(c, w_mod, b_mod.reshape(1, N))


def _in_proj_kernel(x_hbm, mod_ref, w_ref, b_ref, o_ref, xbuf, u_ref, sem):
    i, j = pl.program_id(1), pl.program_id(2)
    ni = pl.num_programs(1)
    tm = xbuf.shape[0]
    step = pl.program_id(0) * ni + i
    n_steps = pl.num_programs(0) * ni

    def x_copy(s):
        return pltpu.make_async_copy(x_hbm.at[s // ni, pl.ds((s % ni) * tm, tm)], xbuf, sem)

    @pl.when(j == 0)
    def _():
        @pl.when(step == 0)
        def _():
            x_copy(0).start()

        x_copy(step).wait()
        shift = mod_ref[0, 0:1, :]
        scale = mod_ref[0, 1:2, :]
        for r0 in range(0, tm, ROW_CHUNK):
            u_ref[r0:r0 + ROW_CHUNK, :] = (xbuf[r0:r0 + ROW_CHUNK, :] * (1.0 + scale) + shift).astype(BF16)

    @pl.when((j == 1) & (step + 1 < n_steps))
    def _():
        x_copy(step + 1).start()

    acc = jnp.dot(u_ref[...], w_ref[...], preferred_element_type=F32)
    o_ref[0] = (acc + b_ref[...]).astype(BF16)


def _in_proj(x, mod, w_in_bf, b_in):
    B, S, D = x.shape
    N = w_in_bf.shape[1]
    tm, tn = min(PROJ_TM, S), PROJ_TN
    assert N // tn >= 2
    return pl.pallas_call(
        _in_proj_kernel,
        out_shape=jax.ShapeDtypeStruct((B, S, N), BF16),
        grid=(B, S // tm, N // tn),
        in_specs=[pl.BlockSpec(memory_space=pl.ANY),
                  pl.BlockSpec((1, 6, D), lambda b, i, j: (b, 0, 0)),
                  pl.BlockSpec((D, tn), lambda b, i, j: (0, j)),
                  pl.BlockSpec((1, tn), lambda b, i, j: (0, j))],
        out_specs=pl.BlockSpec((1, tm, tn), lambda b, i, j: (b, i, j)),
        scratch_shapes=[pltpu.VMEM((tm, D), F32), pltpu.VMEM((tm, D), BF16), pltpu.SemaphoreType.DMA(())],
        compiler_params=_cparams(("arbitrary", "arbitrary", "arbitrary")),
        name="in_proj",
    )(x, mod, w_in_bf, b_in.reshape(1, N))


def _bias_kernel(table_ref, o_ref, *, col0, causal):
    h = pl.program_id(0) + col0
    v = pl.program_id(1)
    rows, cols = o_ref.shape[2], o_ref.shape[3]
    qi = lax.broadcasted_iota(I32, (rows, cols), 0)
    kj = lax.broadcasted_iota(I32, (rows, cols), 1)
    if causal:
        dist = qi - kj + v * rows
        valid = dist >= 0
    else:
        dist = qi - kj + rows
        valid = (dist >= 0) & (dist < WINDOW) & jnp.logical_not((v == 1) & (kj < rows))
    n = jnp.maximum(dist, 0)
    max_exact = N_BUCKETS // 2
    nf = jnp.maximum(n, 1).astype(F32)
    large = max_exact + (jnp.log(nf / max_exact) / math.log(MAX_DISTANCE / max_exact)
                         * (N_BUCKETS - max_exact)).astype(I32)
    large = jnp.minimum(large, N_BUCKETS - 1)
    bucket = jnp.where(n < max_exact, n, large)
    val = jnp.zeros((rows, cols), F32)
    for bkt in range(N_BUCKETS):
        val = jnp.where(bucket == bkt, table_ref[bkt, h], val)
    if causal:
        val = val - table_ref[N_BUCKETS - 1, h]
    o_ref[0, 0] = jnp.where(valid, val * LOG2E, NEG_BIG)


def _bias_tiles(rel_bias, col0, n_heads, n_var, rows, cols, causal):
    return pl.pallas_call(
        functools.partial(_bias_kernel, col0=col0, causal=causal),
        out_shape=jax.ShapeDtypeStruct((n_heads, n_var, rows, cols), F32),
        grid=(n_heads, n_var),
        in_specs=[pl.BlockSpec(memory_space=pltpu.SMEM)],
        out_specs=pl.BlockSpec((1, 1, rows, cols), lambda h, v: (h, v, 0, 0)),
        compiler_params=_cparams(("arbitrary", "arbitrary")),
        name="bias_tiles",
    )(rel_bias)


def _diff_attn_kernel(q_ref, k_ref, v_ref, bias_ref, lam_ref, g_ref, o_ref,
                      m_ref, l_ref, acc_ref, s_ref, p_ref, a_ref, *, lambda_init):
    t = q_ref.shape[1]
    d = HEAD_DIM_DIFF
    n_ch = 2 * ATT_HEADS
    qi = pl.program_id(2)
    q = (q_ref[0].astype(F32) * (d ** -0.5 * LOG2E)).astype(BF16)

    lv = lam_ref[...]
    lam = (jnp.exp(jnp.sum(lv[0:1] * lv[1:2], axis=-1, keepdims=True))
           - jnp.exp(jnp.sum(lv[2:3] * lv[3:4], axis=-1, keepdims=True)) + lambda_init)

    nt = (((1,), (1,)), ((), ()))
    m_ref[...] = jnp.full(m_ref.shape, NEG_BIG, F32)
    l_ref[...] = jnp.zeros(l_ref.shape, F32)
    acc_ref[...] = jnp.zeros(acc_ref.shape, F32)

    def block(kj, var):
        k = k_ref[0, pl.ds(pl.multiple_of(kj * t, t), t), :]
        v = v_ref[0, pl.ds(pl.multiple_of(kj * t, t), t), :]
        for c in range(n_ch):
            s_ref[c] = lax.dot_general(q[:, c * d:(c + 1) * d], k[:, c * d:(c + 1) * d], nt,
                                       preferred_element_type=F32)
        for c in range(n_ch):
            for r0 in range(0, t, ATT_SLAB):
                rs = slice(r0, r0 + ATT_SLAB)
                ch = []
                for j0 in range(0, t, LANES):
                    sj = s_ref[c, rs, j0:j0 + LANES]
                    if var is not None:
                        sj = sj + bias_ref[c // 2, var, rs, j0:j0 + LANES]
                    ch.append(sj)
                m_old = m_ref[c, rs, :]
                m_new = jnp.maximum(m_old, jnp.max(functools.reduce(jnp.maximum, ch), axis=-1, keepdims=True))
                alpha = jnp.exp2(m_old - m_new)
                ps = [jnp.exp2(sj - m_new) for sj in ch]
                l_ref[c, rs, :] = alpha * l_ref[c, rs, :] + functools.reduce(jnp.add, ps)
                m_ref[c, rs, :] = m_new
                a_ref[c, rs, :] = alpha
                for j, pj in enumerate(ps):
                    p_ref[c, rs, j * LANES:(j + 1) * LANES] = pj.astype(BF16)
            al = a_ref[c]
            hh = c // 2
            acc_ref[c] = (acc_ref[c] * jnp.concatenate([al] * (2 * d // LANES), axis=-1)
                          + jnp.dot(p_ref[c], v[:, hh * 2 * d:(hh + 1) * 2 * d], preferred_element_type=F32))

    def far_step(kj, carry):
        block(kj, None)
        return carry

    def near_step(kj, carry):
        block(kj, qi - kj)
        return carry

    n_far = jnp.maximum(qi - 1, 0)
    lax.fori_loop(0, n_far, far_step, 0)
    lax.fori_loop(n_far, qi + 1, near_step, 0)

    for hh in range(ATT_HEADS):
        l0 = jnp.sum(l_ref[2 * hh], axis=-1, keepdims=True)
        l1 = jnp.sum(l_ref[2 * hh + 1], axis=-1, keepdims=True)
        o = acc_ref[2 * hh] / l0 - lam * (acc_ref[2 * hh + 1] / l1)
        o = o * lax.rsqrt(jnp.mean(jnp.square(o), axis=-1, keepdims=True) + RMS_EPS) * g_ref[...]
        o_ref[0, :, hh * 2 * d:(hh + 1) * 2 * d] = (o * (1.0 - lambda_init)).astype(BF16)


def _diff_attn(proj, bias, lam_vecs, subln_g, n_heads, lambda_init):
    B, S, _ = proj.shape
    t = min(ATT_TILE, S)
    w = 2 * HEAD_DIM_DIFF
    nh = ATT_HEADS
    assert n_heads % nh == 0
    ng = n_heads // nh
    return pl.pallas_call(
        functools.partial(_diff_attn_kernel, lambda_init=lambda_init),
        out_shape=jax.ShapeDtypeStruct((B, S, n_heads * w), BF16),
        grid=(B, ng, S // t),
        in_specs=[pl.BlockSpec((1, t, nh * w), lambda b, h, i: (b, i, h)),
                  pl.BlockSpec((1, S, nh * w), lambda b, h, i: (b, 0, ng + h)),
                  pl.BlockSpec((1, S, nh * w), lambda b, h, i: (b, 0, 2 * ng + h)),
                  pl.BlockSpec((nh, 2, t, t), lambda b, h, i: (h, 0, 0, 0)),
                  pl.BlockSpec((4, HEAD_DIM_DIFF), lambda b, h, i: (0, 0)),
                  pl.BlockSpec((1, w), lambda b, h, i: (0, 0))],
        out_specs=pl.BlockSpec((1, t, nh * w), lambda b, h, i: (b, i, h)),
        scratch_shapes=[pltpu.VMEM((2 * nh, t, LANES), F32), pltpu.VMEM((2 * nh, t, LANES), F32),
                        pltpu.VMEM((2 * nh, t, w), F32), pltpu.VMEM((2 * nh, t, t), F32),
                        pltpu.VMEM((2 * nh, t, t), BF16), pltpu.VMEM((2 * nh, t, LANES), F32)],
        compiler_params=_cparams(("arbitrary", "arbitrary", "arbitrary")),
        name="diff_attn",
    )(proj, proj, proj, bias, lam_vecs, subln_g.reshape(1, w))


def _swa_kernel(sink_ref, q_ref, kc_ref, vc_ref, kp_ref, vp_ref, bias_ref, o_ref, *, n_kv):
    d = HEAD_DIM_SWA
    var = jnp.where(pl.program_id(1) == 0, 1, 0)
    q = (q_ref[0].astype(F32) * (d ** -0.5 * LOG2E)).astype(BF16)
    kk = jnp.concatenate([kp_ref[0], kc_ref[0]], axis=0)
    vv = jnp.concatenate([vp_ref[0], vc_ref[0]], axis=0)
    lo_half = lax.broadcasted_iota(I32, (kk.shape[0], LANES), 1) < d
    zero = jnp.zeros((kk.shape[0], LANES), BF16)
    nt = (((1,), (1,)), ((), ()))

    def halves(x2, e):
        swapped = jnp.concatenate([x2[:, d:], x2[:, :d]], axis=1)
        own_lo = x2 if e == 0 else swapped
        own_hi = swapped if e == 0 else x2
        return jnp.where(lo_half, own_lo, zero), jnp.where(lo_half, zero, own_hi)

    for h in range(n_kv):
        pr, e = divmod(h, 2)
        k_lo, k_hi = halves(kk[:, pr * LANES:(pr + 1) * LANES], e)
        v_lo, v_hi = halves(vv[:, pr * LANES:(pr + 1) * LANES], e)
        heads = [h * SWA_GROUP + g for g in range(SWA_GROUP)]
        sinks = [sink_ref[hq] * LOG2E for hq in heads]
        s = [lax.dot_general(q[:, (hq // 2) * LANES:(hq // 2 + 1) * LANES], k_lo if hq % 2 == 0 else k_hi, nt,
                             preferred_element_type=F32) + bias_ref[hq, var] for hq in heads]
        m = [jnp.maximum(jnp.max(s_g, axis=-1, keepdims=True), sk) for s_g, sk in zip(s, sinks)]
        p = [jnp.exp2(s_g - m_g) for s_g, m_g in zip(s, m)]
        denom = [jnp.sum(p_g, axis=-1, keepdims=True) + jnp.exp2(sk - m_g) for p_g, m_g, sk in zip(p, m, sinks)]
        pn = [(p_g * (1.0 / d_g)).astype(BF16) for p_g, d_g in zip(p, denom)]
        for gp in range(SWA_GROUP // 2):
            col = (heads[2 * gp] // 2) * LANES
            o = (jnp.dot(pn[2 * gp], v_lo, preferred_element_type=F32)
                 + jnp.dot(pn[2 * gp + 1], v_hi, preferred_element_type=F32))
            o_ref[0, :, col:col + LANES] = o.astype(BF16)


def _swa_attn(proj, bias, sinks, n_heads_diff, n_heads_swa):
    B, S, _ = proj.shape
    n_kv = n_heads_swa // SWA_GROUP
    r = WINDOW
    qw = n_heads_swa * HEAD_DIM_SWA
    kw = n_kv * HEAD_DIM_SWA
    q_blk = (3 * n_heads_diff * 2 * HEAD_DIM_DIFF) // qw
    k_blk = (3 * n_heads_diff * 2 * HEAD_DIM_DIFF + qw) // kw
    return pl.pallas_call(
        functools.partial(_swa_kernel, n_kv=n_kv),
        out_shape=jax.ShapeDtypeStruct((B, S, qw), BF16),
        grid=(B, S // r),
        in_specs=[pl.BlockSpec(memory_space=pltpu.SMEM),
                  pl.BlockSpec((1, r, qw), lambda b, i: (b, i, q_blk)),
                  pl.BlockSpec((1, r, kw), lambda b, i: (b, i, k_blk)),
                  pl.BlockSpec((1, r, kw), lambda b, i: (b, i, k_blk + 1)),
                  pl.BlockSpec((1, r, kw), lambda b, i: (b, jnp.maximum(i - 1, 0), k_blk)),
                  pl.BlockSpec((1, r, kw), lambda b, i: (b, jnp.maximum(i - 1, 0), k_blk + 1)),
                  pl.BlockSpec((n_heads_swa, 2, r, 2 * r), lambda b, i: (0, 0, 0, 0))],
        out_specs=pl.BlockSpec((1, r, qw), lambda b, i: (b, i, 0)),
        compiler_params=_cparams(("arbitrary", "arbitrary")),
        name="swa_attn",
    )(sinks, proj, proj, proj, proj, proj, bias)


def _pack_bf16_pair(lo, hi):
    lo_b = pltpu.bitcast(lo.astype(BF16).astype(F32), jnp.uint32) >> 16
    hi_b = pltpu.bitcast(hi.astype(BF16).astype(F32), jnp.uint32) & jnp.uint32(0xFFFF0000)
    return pltpu.bitcast(lo_b | hi_b, I32)


def _unpack_bf16_pair(w):
    wu = pltpu.bitcast(w, jnp.uint32)
    lo = pltpu.bitcast(wu << 16, F32).astype(BF16)
    hi = pltpu.bitcast(wu & jnp.uint32(0xFFFF0000), F32).astype(BF16)
    return lo, hi


def _out_proj_kernel(ad_ref, as_ref, w_ref, x_hbm, mod_ref, bo_ref, g_ref, b_ref, wr_ref, br_ref,
                     x1_hbm, u2_hbm, lg_ref, acc_ref, xbuf, x1buf, u2buf, x_sem, o_sem, u_sem, *, n_kd):
    b, i, k = pl.program_id(0), pl.program_id(1), pl.program_id(2)
    nk = pl.num_programs(2)
    tm = acc_ref.shape[0]
    n_chunks = tm // ROW_CHUNK

    def rows(c):
        return pl.ds(i * tm + c * ROW_CHUNK, ROW_CHUNK)

    def x_copy(c):
        return pltpu.make_async_copy(x_hbm.at[b, rows(c)], xbuf.at[c % 2], x_sem.at[c % 2])

    def o_copy(c):
        return pltpu.make_async_copy(x1buf.at[c % 2], x1_hbm.at[b, rows(c)], o_sem.at[c % 2])

    def u_copy(c):
        return pltpu.make_async_copy(u2buf.at[c % 2], u2_hbm.at[b, rows(c)], u_sem.at[c % 2])

    @pl.when(k == 0)
    def _():
        x_copy(0).start()
        acc_ref[...] = jnp.dot(ad_ref[0], w_ref[...], preferred_element_type=F32)

    @pl.when((k > 0) & (k < n_kd))
    def _():
        acc_ref[...] += jnp.dot(ad_ref[0], w_ref[...], preferred_element_type=F32)

    @pl.when(k >= n_kd)
    def _():
        acc_ref[...] += jnp.dot(as_ref[0], w_ref[...], preferred_element_type=F32)

    @pl.when(k == nk - 1)
    def _():
        gate1 = mod_ref[0, 2:3, :]
        shift2 = mod_ref[0, 3:4, :]
        scale2 = mod_ref[0, 4:5, :]
        half = acc_ref.shape[1] // 2
        for c in range(n_chunks):
            rs = slice(c * ROW_CHUNK, (c + 1) * ROW_CHUNK)
            x_copy(c).wait()
            if c + 1 < n_chunks:
                x_copy(c + 1).start()
            if c >= 2:
                o_copy(c - 2).wait()
                u_copy(c - 2).wait()
            a = acc_ref[rs, :] + bo_ref[...]
            h = DEEPNORM_ALPHA * xbuf[c % 2] + (1.0 + gate1) * a
            mu = jnp.mean(h, axis=-1, keepdims=True)
            hc = h - mu
            var = jnp.mean(jnp.square(hc), axis=-1, keepdims=True)
            x1 = hc * lax.rsqrt(var + LN_EPS) * g_ref[...] + b_ref[...]
            x1buf[c % 2] = x1
            u2 = x1 * (1.0 + scale2) + shift2
            lg_ref[0, rs, :] = jnp.dot(u2.astype(BF16), wr_ref[...], preferred_element_type=F32) + br_ref[...]
            u2buf[c % 2] = _pack_bf16_pair(u2[:, :half], u2[:, half:])
            o_copy(c).start()
            u_copy(c).start()
        for c in range(max(n_chunks - 2, 0), n_chunks):
            o_copy(c).wait()
            u_copy(c).wait()


def _out_proj(a_diff, a_swa, w_out_bf, x, mod, b_out, ln_g, ln_b, w_router_bf, b_router):
    B, S, D = x.shape
    E = w_router_bf.shape[1]
    tm, tk = min(OUT_TM, S), OUT_TK
    n_kd = a_diff.shape[2] // tk
    n_ks = a_swa.shape[2] // tk
    row = lambda b, i, k: (0, 0)
    return pl.pallas_call(
        functools.partial(_out_proj_kernel, n_kd=n_kd),
        out_shape=(jax.ShapeDtypeStruct((B, S, D), F32),
                   jax.ShapeDtypeStruct((B, S, D // 2), I32),
                   jax.ShapeDtypeStruct((B, S, E), F32)),
        grid=(B, S // tm, n_kd + n_ks),
        in_specs=[pl.BlockSpec((1, tm, tk), lambda b, i, k: (b, i, jnp.minimum(k, n_kd - 1))),
                  pl.BlockSpec((1, tm, tk), lambda b, i, k: (b, i, jnp.maximum(k - n_kd, 0))),
                  pl.BlockSpec((tk, D), lambda b, i, k: (k, 0)),
                  pl.BlockSpec(memory_space=pl.ANY),
                  pl.BlockSpec((1, 6, D), lambda b, i, k: (b, 0, 0)),
                  pl.BlockSpec((1, D), row), pl.BlockSpec((1, D), row), pl.BlockSpec((1, D), row),
                  pl.BlockSpec((D, E), row), pl.BlockSpec((1, E), row)],
        out_specs=(pl.BlockSpec(memory_space=pl.ANY),
                   pl.BlockSpec(memory_space=pl.ANY),
                   pl.BlockSpec((1, tm, E), lambda b, i, k: (b, i, 0))),
        scratch_shapes=[pltpu.VMEM((tm, D), F32),
                        pltpu.VMEM((2, ROW_CHUNK, D), F32), pltpu.VMEM((2, ROW_CHUNK, D), F32),
                        pltpu.VMEM((2, ROW_CHUNK, D // 2), I32),
                        pltpu.SemaphoreType.DMA((2,)), pltpu.SemaphoreType.DMA((2,)),
                        pltpu.SemaphoreType.DMA((2,))],
        compiler_params=_cparams(("arbitrary", "arbitrary", "arbitrary")),
        name="out_proj",
    )(a_diff, a_swa, w_out_bf, x, mod, b_out.reshape(1, D), ln_g.reshape(1, D), ln_b.reshape(1, D),
      w_router_bf, b_router.reshape(1, E))


def _router_kernel(lg_ref, idx_ref, gate_ref):
    l = lg_ref[...]
    n, e = l.shape
    eidx = lax.broadcasted_iota(I32, (n, e), 1)
    kcol = lax.broadcasted_iota(I32, (n, TOP_K), 1)
    idx_out = jnp.zeros((n, TOP_K), I32)
    val_out = jnp.zeros((n, TOP_K), F32)
    for r in range(TOP_K):
        m = jnp.max(l, axis=-1, keepdims=True)
        i = jnp.min(jnp.where(l == m, eidx, e), axis=-1, keepdims=True)
        idx_out = jnp.where(kcol == r, i, idx_out)
        val_out = jnp.where(kcol == r, m, val_out)
        l = jnp.where(eidx == i, -jnp.inf, l)
    ex = jnp.exp(val_out - jnp.max(val_out, axis=-1, keepdims=True))
    idx_ref[...] = idx_out
    gate_ref[...] = ex / jnp.sum(ex, axis=-1, keepdims=True)


def _router(logits):
    T, E = logits.shape
    tr = min(ROUTER_TR, T)
    return pl.pallas_call(
        _router_kernel,
        out_shape=(jax.ShapeDtypeStruct((T, TOP_K), I32), jax.ShapeDtypeStruct((T, TOP_K), F32)),
        grid=(T // tr,),
        in_specs=[pl.BlockSpec((tr, E), lambda i: (i, 0))],
        out_specs=(pl.BlockSpec((tr, TOP_K), lambda i: (i, 0)), pl.BlockSpec((tr, TOP_K), lambda i: (i, 0))),
        compiler_params=_cparams(("arbitrary",)),
        name="router",
    )(logits)


def _routing_tables(top_idx, n_experts, tm, sub, n_tiles, tq):
    T = top_idx.shape[0]
    flat_e = top_idx.reshape(-1)
    onehot = (flat_e[:, None] == jnp.arange(n_experts, dtype=I32)[None, :]).astype(I32)
    csum = jnp.cumsum(onehot, axis=0)
    rank = jnp.sum(csum * onehot, axis=1) - 1
    counts = csum[-1]
    tiles_per = (counts + tm - 1) // tm
    tile_end = jnp.cumsum(tiles_per)
    tile_start = tile_end - tiles_per
    n_used = tile_end[-1]
    subs_per = (counts + sub - 1) // sub
    y_start = (jnp.cumsum(subs_per) - subs_per) * sub
    total_sub = jnp.sum(subs_per)
    pos = y_start[flat_e] + rank
    tile_ids = jnp.arange(n_tiles, dtype=I32)
    tile_expert = jnp.sum((tile_ids[:, None] >= tile_end[None, :]).astype(I32), axis=1)
    last_expert = jnp.max(jnp.where(counts > 0, jnp.arange(n_experts, dtype=I32), 0))
    tile_expert = jnp.where(tile_ids < n_used, tile_expert, last_expert).astype(I32)
    first_in_expert = (tile_ids - tile_start[tile_expert]) * tm
    tile_rows = jnp.where(tile_ids < n_used, jnp.clip(counts[tile_expert] - first_in_expert, 0, tm), 0)
    tile_sub = ((tile_rows + sub - 1) // sub).astype(I32)
    first_pair = (jnp.cumsum(counts) - counts)[tile_expert] + first_in_expert
    first_pair = jnp.clip(first_pair, 0, T * TOP_K).astype(I32)
    tile_yrow = (y_start[tile_expert] + first_in_expert).astype(I32)
    order = jnp.argsort(flat_e, stable=True).astype(I32)
    sorted_tok = jnp.concatenate([order // TOP_K, jnp.zeros((_idx_len(tm),), I32)])
    pos_steps = pos.reshape(T // tq, tq, TOP_K).transpose(0, 2, 1).reshape(T // tq, TOP_K * tq)
    return (sorted_tok, first_pair, tile_expert, tile_sub, n_used.reshape(1).astype(I32), tile_yrow,
            total_sub.reshape(1).astype(I32), pos_steps.astype(I32))


def _sub_block_loops(nsub, n_sub_max, sub, compute, zero):
    def pair_body(s, carry):
        compute(pl.multiple_of(s * 2 * sub, 2 * sub), 2 * sub)
        return carry

    lax.fori_loop(0, nsub // 2, pair_body, 0)

    @pl.when(nsub % 2 == 1)
    def _():
        compute(pl.multiple_of((nsub - 1) * sub, sub), sub)

    def zero_body(s, carry):
        zero(pl.multiple_of(s * sub, sub), sub)
        return carry

    lax.fori_loop(nsub, n_sub_max, zero_body, 0)


def _idx_len(tm):
    return -(-(tm + LANES) // 1024) * 1024


def _moe_up_kernel(te_ref, ts_ref, fp_ref, tok_hbm, u2_hbm, wg_ref, wl_ref, bg_ref, bl_ref, o_ref,
                   idx_smem, gbuf, xs_ref, wcat_ref, idx_sem, row_sem):
    t = pl.program_id(0)
    j = pl.program_id(1)
    n_tiles = pl.num_programs(0)
    tm = xs_ref.shape[0]
    half = gbuf.shape[1]
    tf = o_ref.shape[1]
    nsub = ts_ref[t]
    il = _idx_len(tm)

    def idx_copy(tile, slot):
        start = pl.multiple_of((fp_ref[tile] // LANES) * LANES, LANES)
        return pltpu.make_async_copy(tok_hbm.at[pl.ds(start, il)], idx_smem.at[pl.ds(slot * il, il)],
                                     idx_sem.at[slot])

    def issue_rows(tile, slot):
        first = slot * il + fp_ref[tile] % LANES

        def body(g, carry):
            base = pl.multiple_of(g * DMA_UNROLL, DMA_UNROLL)
            for u in range(DMA_UNROLL):
                tok = idx_smem[first + base + u]
                pltpu.make_async_copy(u2_hbm.at[pl.ds(tok, 1)], gbuf.at[pl.ds(base + u, 1)], row_sem).start()
            return carry
        lax.fori_loop(0, ts_ref[tile] * (MOE_SUB // DMA_UNROLL), body, 0)

    def wait_sub_block(s, carry):
        pltpu.make_async_copy(u2_hbm.at[pl.ds(0, MOE_SUB)], gbuf.at[pl.ds(0, MOE_SUB)], row_sem).wait()
        return carry

    def unpack_sub_block(s, carry):
        r0 = pl.multiple_of(s * MOE_SUB, MOE_SUB)
        lo, hi = _unpack_bf16_pair(gbuf[pl.ds(r0, MOE_SUB), :])
        xs_ref[pl.ds(r0, MOE_SUB), :half] = lo
        xs_ref[pl.ds(r0, MOE_SUB), half:] = hi
        return carry

    @pl.when(j == 0)
    def _():
        slot = t % 2

        @pl.when(t == 0)
        def _():
            idx_copy(0, 0).start()
            idx_copy(0, 0).wait()
            issue_rows(0, 0)

            @pl.when(n_tiles > 1)
            def _():
                idx_copy(1, 1).start()

        lax.fori_loop(0, nsub, wait_sub_block, 0)
        lax.fori_loop(0, nsub, unpack_sub_block, 0)

        @pl.when(t + 1 < n_tiles)
        def _():
            idx_copy(t + 1, 1 - slot).wait()
            issue_rows(t + 1, 1 - slot)

        @pl.when(t + 2 < n_tiles)
        def _():
            idx_copy(t + 2, slot).start()

    @pl.when(nsub > 0)
    def _():
        wcat_ref[:, :tf] = wg_ref[0].astype(BF16)
        wcat_ref[:, tf:] = wl_ref[0].astype(BF16)

    def compute(r0, rows):
        h = jnp.dot(xs_ref[pl.ds(r0, rows), :], wcat_ref[...], preferred_element_type=F32)
        glu = jnp.minimum(h[:, :tf] + bg_ref[0], SWIGLU_LIMIT)
        lin = jnp.clip(h[:, tf:] + bl_ref[0], -SWIGLU_LIMIT, SWIGLU_LIMIT)
        act = glu * (1.0 / (1.0 + jnp.exp(-SWIGLU_ALPHA * glu))) * (lin + 1.0)
        o_ref[pl.ds(r0, rows), :] = act.astype(BF16)

    def zero(r0, rows):
        o_ref[pl.ds(r0, rows), :] = jnp.zeros((rows, tf), BF16)

    _sub_block_loops(nsub, tm // MOE_SUB, MOE_SUB, compute, zero)


def _moe_up(tile_expert, tile_sub, first_pair, sorted_tok, u2p, w1, b1, tm):
    E, D, F2 = w1.shape
    F = F2 // 2
    n_tiles = tile_expert.shape[0]
    tf = min(MOE_TF, F)
    nj = F // tf

    def jj(t, j, ts):
        return jnp.where(ts[t] > 0, j, nj - 1)

    grid_spec = pltpu.PrefetchScalarGridSpec(
        num_scalar_prefetch=3,
        grid=(n_tiles, nj),
        in_specs=[pl.BlockSpec(memory_space=pl.ANY),
                  pl.BlockSpec(memory_space=pl.ANY),
                  pl.BlockSpec((1, D, tf), lambda t, j, te, ts, fp: (te[t], 0, jj(t, j, ts))),
                  pl.BlockSpec((1, D, tf), lambda t, j, te, ts, fp: (te[t], 0, nj + jj(t, j, ts))),
                  pl.BlockSpec((1, 1, tf), lambda t, j, te, ts, fp: (te[t], 0, jj(t, j, ts))),
                  pl.BlockSpec((1, 1, tf), lambda t, j, te, ts, fp: (te[t], 0, nj + jj(t, j, ts)))],
        out_specs=pl.BlockSpec((tm, tf), lambda t, j, te, ts, fp: (t, j)),
        scratch_shapes=[pltpu.SMEM((2 * _idx_len(tm),), I32),
                        pltpu.VMEM((tm, D // 2), I32),
                        pltpu.VMEM((tm, D), BF16),
                        pltpu.VMEM((D, 2 * tf), BF16),
                        pltpu.SemaphoreType.DMA((2,)),
                        pltpu.SemaphoreType.DMA(())],
    )
    return pl.pallas_call(
        _moe_up_kernel,
        out_shape=jax.ShapeDtypeStruct((n_tiles * tm, F), BF16),
        grid_spec=grid_spec,
        compiler_params=_cparams(("arbitrary", "arbitrary")),
        name="moe_up",
    )(tile_expert, tile_sub, first_pair, sorted_tok, u2p, w1, w1, b1.reshape(E, 1, F2), b1.reshape(E, 1, F2))


def _moe_down_kernel(te_ref, ts_ref, nu_ref, yr_ref, tot_ref, a_ref, w_ref, b_ref, y_hbm, wb_ref, stage, o_sem, z_sem):
    t, j = pl.program_id(0), pl.program_id(1)
    n_tiles, nj = pl.num_programs(0), pl.num_programs(1)
    nsub = ts_ref[t]
    n_pairs = nsub // 2
    hw = stage.shape[2]
    sub = MOE_SUB
    col = pl.multiple_of(j * hw, hw)

    def out_copy(slot, row0, rows):
        return pltpu.make_async_copy(stage.at[slot, pl.ds(0, rows)],
                                     y_hbm.at[pl.ds(pl.multiple_of(row0, sub), rows), pl.ds(col, hw)], o_sem.at[slot])

    def packed(r0, rows):
        y = jnp.dot(a_ref[pl.ds(r0, rows), :], wb_ref[...], preferred_element_type=F32) + b_ref[0]
        return _pack_bf16_pair(y[:, :hw], y[:, hw:])

    @pl.when(nsub > 0)
    def _():
        wb_ref[...] = w_ref[0].astype(BF16)

    def pair_body(s, carry):
        slot = s % 2

        @pl.when(s >= 2)
        def _():
            out_copy(slot, 0, 2 * sub).wait()

        r0 = pl.multiple_of(s * 2 * sub, 2 * sub)
        stage[slot] = packed(r0, 2 * sub)
        out_copy(slot, yr_ref[t] + r0, 2 * sub).start()
        return carry

    lax.fori_loop(0, n_pairs, pair_body, 0)

    @pl.when(n_pairs >= 2)
    def _():
        out_copy(n_pairs % 2, 0, 2 * sub).wait()

    @pl.when(n_pairs >= 1)
    def _():
        out_copy((n_pairs - 1) % 2, 0, 2 * sub).wait()

    @pl.when(nsub % 2 == 1)
    def _():
        r0 = pl.multiple_of((nsub - 1) * sub, sub)
        stage[0, :sub] = packed(r0, sub)
        out_copy(0, yr_ref[t] + r0, sub).start()
        out_copy(0, 0, sub).wait()

    @pl.when((t == n_tiles - 1) & (j == nj - 1))
    def _():
        n_zero = y_hbm.shape[0] // sub - tot_ref[0]
        stage[0, :sub] = jnp.zeros((sub, hw), I32)

        def zero_copy(z, c):
            return pltpu.make_async_copy(
                stage.at[0, pl.ds(0, sub)],
                y_hbm.at[pl.ds(pl.multiple_of((tot_ref[0] + z) * sub, sub), sub), pl.ds(c * hw, hw)], z_sem)

        def start_body(z, carry):
            for c in range(y_hbm.shape[1] // hw):
                zero_copy(z, c).start()
            return carry

        def wait_body(z, carry):
            for c in range(y_hbm.shape[1] // hw):
                zero_copy(0, c).wait()
            return carry

        lax.fori_loop(0, n_zero, start_body, 0)
        lax.fori_loop(0, n_zero, wait_body, 0)


def _moe_down(tile_expert, tile_sub, n_used, tile_yrow, total_sub, act, w2, b2, tm, y_rows):
    E, F, D = w2.shape
    rows = act.shape[0]
    n_tiles = rows // tm
    tn = min(MOE_TN, D)
    nj = D // tn

    def jj(t, j, ts):
        return jnp.where(ts[t] > 0, j, nj - 1)

    def tt(t, nu):
        return jnp.minimum(t, nu[0] - 1)

    grid_spec = pltpu.PrefetchScalarGridSpec(
        num_scalar_prefetch=5,
        grid=(n_tiles, nj),
        in_specs=[pl.BlockSpec((tm, F), lambda t, j, te, ts, nu, yr, tot: (tt(t, nu), 0)),
                  pl.BlockSpec((1, F, tn), lambda t, j, te, ts, nu, yr, tot: (te[t], 0, jj(t, j, ts))),
                  pl.BlockSpec((1, 1, tn), lambda t, j, te, ts, nu, yr, tot: (te[t], 0, jj(t, j, ts)))],
        out_specs=pl.BlockSpec(memory_space=pl.ANY),
        scratch_shapes=[pltpu.VMEM((F, tn), BF16), pltpu.VMEM((2, 2 * MOE_SUB, tn // 2), I32),
                        pltpu.SemaphoreType.DMA((2,)), pltpu.SemaphoreType.DMA(())],
    )
    return pl.pallas_call(
        _moe_down_kernel,
        out_shape=jax.ShapeDtypeStruct((y_rows, D // 2), I32),
        grid_spec=grid_spec,
        compiler_params=_cparams(("arbitrary", "arbitrary")),
        name="moe_down",
    )(tile_expert, tile_sub, n_used, tile_yrow, total_sub, act, w2, b2.reshape(E, 1, D))


def _combine_kernel(pos_hbm, y_hbm, gate_ref, x1_ref, mod_ref, g_ref, b_ref, o_ref,
                    idx_smem, ybuf, idx_sem, row_sem, *, tn):
    s = pl.program_id(0)
    n_steps = pl.num_programs(0)
    tq = x1_ref.shape[0]
    n_rows = TOP_K * tq
    slot = s % 2
    nxt = 1 - slot

    def idx_copy(step, sl):
        return pltpu.make_async_copy(pos_hbm.at[step], idx_smem.at[pl.ds(sl * n_rows, n_rows)], idx_sem.at[sl])

    def issue_rows(sl):
        def body(g, carry):
            base = pl.multiple_of(g * DMA_UNROLL, DMA_UNROLL)
            for u in range(DMA_UNROLL):
                row = idx_smem[sl * n_rows + base + u]
                pltpu.make_async_copy(y_hbm.at[pl.ds(row, 1)], ybuf.at[pl.ds(sl * n_rows + base + u, 1)],
                                      row_sem.at[sl]).start()
            return carry
        lax.fori_loop(0, n_rows // DMA_UNROLL, body, 0)

    @pl.when(s == 0)
    def _():
        idx_copy(0, 0).start()
        idx_copy(0, 0).wait()
        issue_rows(0)

        @pl.when(n_steps > 1)
        def _():
            idx_copy(1, 1).start()

    @pl.when(s + 1 < n_steps)
    def _():
        idx_copy(s + 1, nxt).wait()
        issue_rows(nxt)

    pltpu.make_async_copy(y_hbm.at[pl.ds(0, n_rows)], ybuf.at[pl.ds(slot * n_rows, n_rows)],
                          row_sem.at[slot]).wait()

    @pl.when(s + 2 < n_steps)
    def _():
        idx_copy(s + 2, slot).start()

    gate2 = mod_ref[0, 5:6, :]
    hw = tn // 2
    for r0 in range(0, tq, ROW_CHUNK):
        f_lo = f_hi = None
        for k in range(TOP_K):
            wu = pltpu.bitcast(ybuf[pl.ds(slot * n_rows + k * tq + r0, ROW_CHUNK), :], jnp.uint32)
            gk = gate_ref[r0:r0 + ROW_CHUNK, k:k + 1]
            lo = gk * pltpu.bitcast(wu << 16, F32)
            hi = gk * pltpu.bitcast(wu & jnp.uint32(0xFFFF0000), F32)
            f_lo = lo if f_lo is None else f_lo + lo
            f_hi = hi if f_hi is None else f_hi + hi
        parts = []
        for c0 in range(0, f_lo.shape[1], hw):
            parts += [f_lo[:, c0:c0 + hw], f_hi[:, c0:c0 + hw]]
        f = jnp.concatenate(parts, axis=-1)
        h = DEEPNORM_ALPHA * x1_ref[r0:r0 + ROW_CHUNK, :] + (1.0 + gate2) * f
        mu = jnp.mean(h, axis=-1, keepdims=True)
        hc = h - mu
        var = jnp.mean(jnp.square(hc), axis=-1, keepdims=True)
        o_ref[r0:r0 + ROW_CHUNK, :] = hc * lax.rsqrt(var + LN_EPS) * g_ref[...] + b_ref[...]


def _combine(pos_steps, y, gates, x1, mod, ln_g, ln_b, seq, tn):
    T, D = x1.shape
    n_steps = pos_steps.shape[0]
    tq = T // n_steps
    steps_per_batch = seq // tq
    row = lambda s: (0, 0)
    return pl.pallas_call(
        functools.partial(_combine_kernel, tn=tn),
        out_shape=jax.ShapeDtypeStruct((T, D), F32),
        grid=(n_steps,),
        in_specs=[pl.BlockSpec(memory_space=pl.ANY),
                  pl.BlockSpec(memory_space=pl.ANY),
                  pl.BlockSpec((tq, TOP_K), lambda s: (s, 0)),
                  pl.BlockSpec((tq, D), lambda s: (s, 0)),
                  pl.BlockSpec((1, 6, D), lambda s: (s // steps_per_batch, 0, 0)),
                  pl.BlockSpec((1, D), row), pl.BlockSpec((1, D), row)],
        out_specs=pl.BlockSpec((tq, D), lambda s: (s, 0)),
        scratch_shapes=[pltpu.SMEM((2 * TOP_K * tq,), I32),
                        pltpu.VMEM((2 * TOP_K * tq, D // 2), I32),
                        pltpu.SemaphoreType.DMA((2,)),
                        pltpu.SemaphoreType.DMA((2,))],
        compiler_params=_cparams(("arbitrary",)),
        name="combine",
    )(pos_steps, y, gates, x1, mod, ln_g.reshape(1, D), ln_b.reshape(1, D))


def kernel(x, c, rel_bias, w_mod, b_mod, w_in, b_in, lambda_q1, lambda_k1, lambda_q2, lambda_k2, subln_g, sinks,
           w_out, b_out, ln1_g, ln1_b, w_router, b_router, w1, b1, w2, b2, ln2_g, ln2_b):
    B, S, D = x.shape
    T = B * S
    E = w_router.shape[2]
    n_heads_diff = (D // 2) // (2 * HEAD_DIM_DIFF)
    n_heads_swa = (D // 2) // HEAD_DIM_SWA
    t_att = min(ATT_TILE, S)
    for l in range(DEPTH):
        lambda_init = 0.8 - 0.6 * math.exp(-0.3 * l)
        mod = _mod(c, w_mod[l], b_mod[l]).reshape(B, 6, D)
        proj = _in_proj(x, mod, w_in[l].astype(BF16), b_in[l])
        bias_d = _bias_tiles(rel_bias, 0, n_heads_diff, 2, t_att, t_att, True)
        bias_s = _bias_tiles(rel_bias, n_heads_diff, n_heads_swa, 2, WINDOW, 2 * WINDOW, False)
        lam_vecs = jnp.stack([lambda_q1[l], lambda_k1[l], lambda_q2[l], lambda_k2[l]]).astype(F32)
        a_diff = _diff_attn(proj, bias_d, lam_vecs, subln_g[l], n_heads_diff, lambda_init)
        a_swa = _swa_attn(proj, bias_s, sinks[l], n_heads_diff, n_heads_swa)
        x1, u2p, logits = _out_proj(a_diff, a_swa, w_out[l].astype(BF16), x, mod, b_out[l], ln1_g[l], ln1_b[l],
                                    w_router[l].astype(BF16), b_router[l])
        top_idx, gates = _router(logits.reshape(T, E))
        tm = MOE_TM
        assert tm % MOE_SUB == 0
        n_tiles = (T * TOP_K) // tm + E
        tq = min(CMB_TQ, S)
        sorted_tok, first_pair, tile_expert, tile_sub, n_used, tile_yrow, total_sub, pos_steps = _routing_tables(
            top_idx, E, tm, MOE_SUB, n_tiles, tq)
        act = _moe_up(tile_expert, tile_sub, first_pair, sorted_tok, u2p.reshape(T, D // 2), w1[l], b1[l], tm)
        y_rows = -(-(T * TOP_K) // MOE_SUB) * MOE_SUB + E * MOE_SUB
        y = _moe_down(tile_expert, tile_sub, n_used, tile_yrow, total_sub, act, w2[l], b2[l], tm, y_rows)
        x = _combine(pos_steps, y, gates, x1.reshape(T, D), mod, ln2_g[l], ln2_b[l], S,
                     min(MOE_TN, D)).reshape(B, S, D)
    return x
```
